```python
import jax, jax.numpy as jnp
from jax import lax
import numpy as np

D_MODEL = 2048
BATCH = 8
SEQ = 4096
DEPTH = 4

GRID_W = 64
CTX_LEN = 256
HEAD_DIM = 128
N_Q_HEADS = D_MODEL // 256
N_KV_HEADS = 2
Q_PER_KV = N_Q_HEADS // N_KV_HEADS
ATTN_WIDTH = N_Q_HEADS * HEAD_DIM
KV_WIDTH = N_KV_HEADS * HEAD_DIM
WINDOW = 128
BLOCK = 128
ROPE_BASE = 10000.0
ROPE_FREQS = HEAD_DIM // 4
RNN_WIDTH = D_MODEL // 2
RNN_BLOCKS = 8
RNN_BLOCK_DIM = RNN_WIDTH // RNN_BLOCKS
CONV_W = 4
CONV_PAD = (2, 1)
RGLRU_C = 8.0
D_FF = ((8 * D_MODEL // 3 + 127) // 128) * 128
N_MOD = 9
EPS = 1e-6
OFF_K = ATTN_WIDTH
OFF_V = OFF_K + KV_WIDTH
OFF_XR = OFF_V + KV_WIDTH
OFF_GR = OFF_XR + RNN_WIDTH
OFF_GA = OFF_GR + RNN_WIDTH
OFF_GB = OFF_GA + D_MODEL
IN_WIDTH = OFF_GB + D_MODEL
SPLITS = (OFF_K, OFF_V, OFF_XR, OFF_GR, OFF_GA, OFF_GB)

kernel_name = "hybrid_rglru_swa_macaron_prefix_dit"


def rms_norm(x, g):
    xf = x.astype(jnp.float32)
    y = xf * lax.rsqrt(jnp.mean(xf * xf, axis=-1, keepdims=True) + EPS)
    return (y * g.astype(jnp.float32)).astype(x.dtype)


def modulate(h, shift, scale):
    return h * (1 + scale) + shift


def swiglu(h, w_up, w_down):
    gate, up = jnp.split(h @ w_up, 2, axis=-1)
    return (jax.nn.silu(gate) * up) @ w_down


def axial_rope_tables(n_tokens):
    rows = n_tokens // GRID_W
    row = jnp.repeat(jnp.arange(rows, dtype=jnp.float32), GRID_W)
    col = jnp.tile(jnp.arange(GRID_W, dtype=jnp.float32), rows)
    inv_freq = ROPE_BASE ** (-jnp.arange(ROPE_FREQS, dtype=jnp.float32) / ROPE_FREQS)
    ang = jnp.stack([row[:, None] * inv_freq, col[:, None] * inv_freq], axis=1)
    return jnp.cos(ang), jnp.sin(ang)


def apply_axial_rope(t, cos, sin):
    shp = t.shape
    tr = t.reshape(*shp[:-1], 2, 2, ROPE_FREQS)
    t1, t2 = tr[..., 0, :], tr[..., 1, :]
    cs = cos[:, None].astype(t.dtype)
    sn = sin[:, None].astype(t.dtype)
    out = jnp.stack([t1 * cs - t2 * sn, t2 * cs + t1 * sn], axis=-2)
    return out.reshape(shp)


def windowed_gqa_with_context(q, k, v, kc, vc, sink):
    bsz, s_len = q.shape[:2]
    nb = s_len // BLOCK
    scale = HEAD_DIM ** -0.5
    qb = q.reshape(bsz, nb, BLOCK, N_KV_HEADS, Q_PER_KV, HEAD_DIM)

    def band(t):
        tp = jnp.pad(t, ((0, 0), (BLOCK, BLOCK), (0, 0), (0, 0)))
        tp = tp.reshape(bsz, nb + 2, BLOCK, N_KV_HEADS, HEAD_DIM)
        return jnp.concatenate([tp[:, :-2], tp[:, 1:-1], tp[:, 2:]], axis=2)

    kw, vw = band(k), band(v)
    s_loc = jnp.einsum('bnqhgd,bnkhd->bnhgqk', qb, kw).astype(jnp.float32) * scale
    qpos = jnp.arange(nb)[:, None] * BLOCK + jnp.arange(BLOCK)[None, :]
    kpos = (jnp.arange(nb)[:, None] - 1) * BLOCK + jnp.arange(3 * BLOCK)[None, :]
    kp = kpos[:, None, :]
    valid = (jnp.abs(kp - qpos[:, :, None]) <= WINDOW) & (kp >= 0) & (kp < s_len)
    s_loc = jnp.where(valid[None, :, None, None], s_loc, -jnp.inf)
    s_ctx = jnp.einsum('bnqhgd,bchd->bnhgqc', qb, kc).astype(jnp.float32) * scale
    sk = sink.astype(jnp.float32).reshape(N_KV_HEADS, Q_PER_KV)[None, None, :, :, None, None]
    m = jnp.maximum(jnp.maximum(s_loc.max(-1, keepdims=True), s_ctx.max(-1, keepdims=True)), sk)
    p_loc = jnp.exp(s_loc - m)
    p_ctx = jnp.exp(s_ctx - m)
    denom = p_loc.sum(-1, keepdims=True) + p_ctx.sum(-1, keepdims=True) + jnp.exp(sk - m)
    o = (jnp.einsum('bnhgqk,bnkhd->bnqhgd', (p_loc / denom).astype(v.dtype), vw)
         + jnp.einsum('bnhgqc,bchd->bnqhgd', (p_ctx / denom).astype(v.dtype), vc))
    return o.reshape(bsz, s_len, ATTN_WIDTH)


def context_attention(qc, kc, vc, sink):
    bsz, c_len = qc.shape[:2]
    qg = qc.reshape(bsz, c_len, N_KV_HEADS, Q_PER_KV, HEAD_DIM)
    s = jnp.einsum('bchgd,bkhd->bhgck', qg, kc).astype(jnp.float32) * (HEAD_DIM ** -0.5)
    sk = sink.astype(jnp.float32).reshape(N_KV_HEADS, Q_PER_KV)[None, :, :, None, None]
    m = jnp.maximum(s.max(-1, keepdims=True), sk)
    p = jnp.exp(s - m)
    denom = p.sum(-1, keepdims=True) + jnp.exp(sk - m)
    o = jnp.einsum('bhgck,bkhd->bchgd', (p / denom).astype(vc.dtype), vc)
    return o.reshape(bsz, c_len, ATTN_WIDTH)


def depthwise_conv(x, w, b):
    y = lax.conv_general_dilated(x, w[:, None, :].astype(x.dtype), window_strides=(1,), padding=[CONV_PAD],
                                 dimension_numbers=('NWC', 'WIO', 'NWC'), feature_group_count=x.shape[-1])
    return y + b


def block_diag(x, w, b):
    xb = x.reshape(*x.shape[:-1], RNN_BLOCKS, RNN_BLOCK_DIM)
    y = jnp.einsum('...nd,nde->...ne', xb, w.astype(jnp.float32))
    return y.reshape(x.shape) + b.astype(jnp.float32)


def rglru_coeffs(u, w, b, lam):
    uf = u.astype(jnp.float32)
    r = jax.nn.sigmoid(block_diag(uf, w[0], b[0]))
    i = jax.nn.sigmoid(block_diag(uf, w[1], b[1]))
    log_a = -RGLRU_C * r * jax.nn.softplus(-lam.astype(jnp.float32))
    a = jnp.exp(log_a)
    bx = jnp.sqrt(-jnp.expm1(2.0 * log_a)) * (i * uf)
    return a, bx


def linear_scan(a, bx, h0, reverse):
    def combine(e1, e2):
        a1, b1 = e1
        a2, b2 = e2
        return a1 * a2, a2 * b1 + b2
    a_cum, b_cum = lax.associative_scan(combine, (a, bx), reverse=reverse, axis=1)
    return a_cum * h0[:, None, :] + b_cum


def bidir_rglru(u_lat, u_ctx, conv_w, conv_b, rg_w, rg_b, rg_lambda, need_ctx):
    ul = depthwise_conv(u_lat, conv_w, conv_b)
    uc = depthwise_conv(u_ctx, conv_w, conv_b)
    h0 = jnp.zeros((u_ctx.shape[0], RNN_WIDTH), jnp.float32)
    lat_out, ctx_out = [], []
    for d, reverse in enumerate((False, True)):
        a_c, b_c = rglru_coeffs(uc, rg_w[d], rg_b[d], rg_lambda[d])
        h_ctx = linear_scan(a_c, b_c, h0, reverse)
        h_end = h_ctx[:, 0] if reverse else h_ctx[:, -1]
        a_l, b_l = rglru_coeffs(ul, rg_w[d], rg_b[d], rg_lambda[d])
        lat_out.append(linear_scan(a_l, b_l, h_end, reverse))
        ctx_out.append(h_ctx)
    y_lat = (lat_out[0] + lat_out[1]).astype(u_lat.dtype)
    y_ctx = (ctx_out[0] + ctx_out[1]).astype(u_ctx.dtype) if need_ctx else None
    return y_lat, y_ctx


def token_mixing(h, hc, w_in, sink, conv_w, conv_b, rg_w, rg_b, rg_lambda, w_o_attn, w_o_rnn, w_out,
                 cos, sin, need_ctx):
    bsz, s_len, _ = h.shape
    c_len = hc.shape[1]
    q, k, v, xr, gr, ga, gb = jnp.split(h @ w_in, SPLITS, axis=-1)
    q = apply_axial_rope(q.reshape(bsz, s_len, N_Q_HEADS, HEAD_DIM), cos, sin)
    k = apply_axial_rope(k.reshape(bsz, s_len, N_KV_HEADS, HEAD_DIM), cos, sin)
    v = v.reshape(bsz, s_len, N_KV_HEADS, HEAD_DIM)
    if need_ctx:
        qc, kc, vc, xrc, grc, gac, gbc = jnp.split(hc @ w_in, SPLITS, axis=-1)
    else:
        kc, vc, xrc = jnp.split(hc @ w_in[:, OFF_K:OFF_GR], (KV_WIDTH, 2 * KV_WIDTH), axis=-1)
    kc = kc.reshape(bsz, c_len, N_KV_HEADS, HEAD_DIM)
    vc = vc.reshape(bsz, c_len, N_KV_HEADS, HEAD_DIM)

    y_attn = windowed_gqa_with_context(q, k, v, kc, vc, sink)
    r_lat, r_ctx = bidir_rglru(xr, xrc, conv_w, conv_b, rg_w, rg_b, rg_lambda, need_ctx)

    def merge(ya, yr, g_r, g_a, g_b):
        y_rnn = yr * jax.nn.gelu(g_r)
        return (jax.nn.sigmoid(g_a) * (ya @ w_o_attn) + jax.nn.sigmoid(g_b) * (y_rnn @ w_o_rnn)) @ w_out

    y = merge(y_attn, r_lat, gr, ga, gb)
    yc = merge(context_attention(qc, kc, vc, sink), r_ctx, grc, gac, gbc) if need_ctx else None
    return y, yc


def setup_inputs(seed: int = 0) -> dict:
    key = jax.random.key(seed)
    ks = jax.random.split(key, 20)
    D = D_MODEL
    f32 = jnp.float32

    def nrm(k, shape, fan_in, gain=1.0):
        return jax.random.normal(k, shape, f32) * (gain * fan_in ** -0.5)

    x = jax.random.normal(ks[0], (BATCH, SEQ, D), f32)
    c = jax.random.normal(ks[1], (BATCH, D), f32)
    ctx = jax.random.normal(ks[2], (BATCH, CTX_LEN, D), f32)
    c_ctx = jax.random.normal(ks[3], (D,), f32)
    w_ada = nrm(ks[4], (DEPTH, D, N_MOD * D), D, 0.5)
    b_ada = 0.01 * jax.random.normal(ks[5], (DEPTH, N_MOD * D), f32)
    norm_g = 1.0 + 0.02 * jax.random.normal(ks[6], (DEPTH, 3, D), f32)
    final_g = 1.0 + 0.02 * jax.random.normal(ks[7], (D,), f32)
    w_ffn_up = nrm(ks[8], (DEPTH, 2, D, 2 * D_FF), D)
    w_ffn_down = nrm(ks[9], (DEPTH, 2, D_FF, D), D_FF)
    w_in = nrm(ks[10], (DEPTH, D, IN_WIDTH), D)
    attn_sink = 0.5 * jax.random.normal(ks[11], (DEPTH, N_Q_HEADS), f32)
    conv_w = nrm(ks[12], (DEPTH, CONV_W, RNN_WIDTH), CONV_W)
    conv_b = 0.01 * jax.random.normal(ks[13], (DEPTH, RNN_WIDTH), f32)
    rg_w = nrm(ks[14], (DEPTH, 2, 2, RNN_BLOCKS, RNN_BLOCK_DIM, RNN_BLOCK_DIM), RNN_BLOCK_DIM)
    rg_b = 0.01 * jax.random.normal(ks[15], (DEPTH, 2, 2, RNN_WIDTH), f32)
    a_pow_c = jax.random.uniform(ks[16], (DEPTH, 2, RNN_WIDTH), f32, minval=0.9, maxval=0.999)
    a0 = a_pow_c ** (1.0 / RGLRU_C)
    rg_lambda = jnp.log(a0) - jnp.log1p(-a0)
    w_o_attn = nrm(ks[17], (DEPTH, ATTN_WIDTH, D), ATTN_WIDTH)
    w_o_rnn = nrm(ks[18], (DEPTH, RNN_WIDTH, D), RNN_WIDTH)
    w_out = nrm(ks[19], (DEPTH, D, D), D)
    return {"x": x, "c": c, "ctx": ctx, "c_ctx": c_ctx, "w_ada": w_ada, "b_ada": b_ada,
            "norm_g": norm_g, "final_g": final_g, "w_ffn_up": w_ffn_up, "w_ffn_down": w_ffn_down,
            "w_in": w_in, "attn_sink": attn_sink, "conv_w": conv_w, "conv_b": conv_b,
            "rg_w": rg_w, "rg_b": rg_b, "rg_lambda": rg_lambda, "w_o_attn": w_o_attn,
            "w_o_rnn": w_o_rnn, "w_out": w_out}


def reference(x, c, ctx, c_ctx, w_ada, b_ada, norm_g, final_g, w_ffn_up, w_ffn_down, w_in, attn_sink,
              conv_w, conv_b, rg_w, rg_b, rg_lambda, w_o_attn, w_o_rnn, w_out):
    s_len = x.shape[1]
    cos, sin = axial_rope_tables(s_len)
    sc = jax.nn.silu(c)
    scc = jax.nn.silu(c_ctx)
    xc = ctx
    for l in range(DEPTH):
        need_ctx = l < DEPTH - 1
        mx = jnp.split((sc @ w_ada[l] + b_ada[l])[:, None, :], N_MOD, axis=-1)
        mc = jnp.split((scc @ w_ada[l] + b_ada[l])[None, None, :], N_MOD, axis=-1)
        x = x + 0.5 * mx[2] * swiglu(modulate(rms_norm(x, norm_g[l, 0]), mx[0], mx[1]),
                                     w_ffn_up[l, 0], w_ffn_down[l, 0])
        xc = xc + 0.5 * mc[2] * swiglu(modulate(rms_norm(xc, norm_g[l, 0]), mc[0], mc[1]),
                                       w_ffn_up[l, 0], w_ffn_down[l, 0])
        h = modulate(rms_norm(x, norm_g[l, 1]), mx[3], mx[4])
        hc = modulate(rms_norm(xc, norm_g[l, 1]), mc[3], mc[4])
        y, yc = token_mixing(h, hc, w_in[l], attn_sink[l], conv_w[l], conv_b[l], rg_w[l], rg_b[l],
                             rg_lambda[l], w_o_attn[l], w_o_rnn[l], w_out[l], cos, sin, need_ctx)
        x = x + mx[5] * y
        x = x + 0.5 * mx[8] * swiglu(modulate(rms_norm(x, norm_g[l, 2]), mx[6], mx[7]),
                                     w_ffn_up[l, 1], w_ffn_down[l, 1])
        if need_ctx:
            xc = xc + mc[5] * yc
            xc = xc + 0.5 * mc[8] * swiglu(modulate(rms_norm(xc, norm_g[l, 2]), mc[6], mc[7]),
                                           w_ffn_up[l, 1], w_ffn_down[l, 1])
    return rms_norm(x, final_g)
```

```python
import functools

import jax
import jax.numpy as jnp
from jax import lax
from jax.experimental import pallas as pl
from jax.experimental.pallas import tpu as pltpu

F32 = jnp.float32
BF16 = jnp.bfloat16

LANES = 128
SUBLANES = 8
VMEM_LIMIT_BYTES = 56 * 1024 * 1024

HEAD_DIM = 128
N_KV_HEADS = 2
WINDOW_BLOCK = 128
GRID_W = 64
ROPE_BASE = 10000.0
ROPE_FREQS = HEAD_DIM // 4
CONV_LEFT = 2
RGLRU_C = 8.0
N_MOD = 9
EPS = 1e-6
MOD_ROWS = 16
NEG_BIG = -1e30

TM = 512
TF = 512
TN_PROJ = 512
TN_ADA = 1024
SCAN_ROWS = 256
NORM_ROWS = 128


def _cparams(sem):
    return pltpu.CompilerParams(dimension_semantics=sem, vmem_limit_bytes=VMEM_LIMIT_BYTES)


def _adaln_kernel(c_ref, w_ref, b_ref, o_ref):
    c = c_ref[...]
    sc = c * jax.nn.sigmoid(c)
    o_ref[...] = jnp.dot(sc, w_ref[...], preferred_element_type=F32,
                         precision=lax.Precision.HIGHEST) + b_ref[...]


def _adaln(c_all, w_ada, b_ada):
    depth, d, n = w_ada.shape
    return pl.pallas_call(
        _adaln_kernel,
        out_shape=jax.ShapeDtypeStruct((depth, MOD_ROWS, n), F32),
        grid=(depth, n // TN_ADA),
        in_specs=[
            pl.BlockSpec((MOD_ROWS, d), lambda l, j: (0, 0)),
            pl.BlockSpec((None, d, TN_ADA), lambda l, j: (l, 0, j)),
            pl.BlockSpec((None, 1, TN_ADA), lambda l, j: (l, 0, j)),
        ],
        out_specs=pl.BlockSpec((None, MOD_ROWS, TN_ADA), lambda l, j: (l, 0, j)),
        compiler_params=_cparams(("parallel", "parallel")),
        name="adaln",
    )(c_all, w_ada, b_ada.reshape(depth, 1, n))


def _norm_mod_store(xn_ref, x_ref, g_ref, sh_ref, sc_ref):
    g = g_ref[...]
    sh = sh_ref[...]
    sc1 = 1.0 + sc_ref[...]

    def body(i, carry):
        r0 = pl.multiple_of(i * NORM_ROWS, NORM_ROWS)
        x = x_ref[pl.ds(r0, NORM_ROWS), :]
        ms = jnp.mean(x * x, axis=-1, keepdims=True)
        y = (x * lax.rsqrt(ms + EPS)) * g
        xn_ref[pl.ds(r0, NORM_ROWS), :] = (y * sc1 + sh).astype(xn_ref.dtype)
        return carry

    lax.fori_loop(0, x_ref.shape[0] // NORM_ROWS, body, 0)


def _mod_spec(layer, k, row_of_tile, d):
    return pl.BlockSpec((None, None, 1, d), lambda i, j: (layer, row_of_tile(i), 0, k))


def _row_of_tile(n_lat_tiles, tiles_per_sample, ctx_row):
    def f(i):
        return jnp.where(i < n_lat_tiles, i // tiles_per_sample, ctx_row)
    return f


def _ffn_kernel(x_ref, g_ref, sh_ref, sc_ref, gt_ref, wg_ref, wu_ref, wd_ref, *refs, final_norm):
    if final_norm:
        fg_ref, o_ref, xn_ref, acc_ref = refs
    else:
        o_ref, xn_ref, acc_ref = refs
    c = pl.program_id(1)

    @pl.when(c == 0)
    def _():
        _norm_mod_store(xn_ref, x_ref, g_ref, sh_ref, sc_ref)
        acc_ref[...] = jnp.zeros_like(acc_ref)

    xn = xn_ref[...]
    h = jnp.dot(xn, wg_ref[...], preferred_element_type=F32)
    u = jnp.dot(xn, wu_ref[...], preferred_element_type=F32)
    a = ((h * jax.nn.sigmoid(h)) * u).astype(BF16)
    acc_ref[...] += jnp.dot(a, wd_ref[...], preferred_element_type=F32)

    @pl.when(c == pl.num_programs(1) - 1)
    def _():
        y = x_ref[...] + (0.5 * gt_ref[...]) * acc_ref[...]
        if final_norm:
            ms = jnp.mean(y * y, axis=-1, keepdims=True)
            y = (y * lax.rsqrt(ms + EPS)) * fg_ref[...]
        o_ref[...] = y


def _ffn(xs, mods, layer, mod_k0, g, wg, wu, wd, n_tiles, row_of_tile, final_g=None):
    m_tot, d = xs.shape
    ff = wg.shape[1]
    final_norm = final_g is not None
    in_specs = [
        pl.BlockSpec((TM, d), lambda i, c: (i, 0)),
        pl.BlockSpec((1, d), lambda i, c: (0, 0)),
        _mod_spec(layer, mod_k0, row_of_tile, d),
        _mod_spec(layer, mod_k0 + 1, row_of_tile, d),
        _mod_spec(layer, mod_k0 + 2, row_of_tile, d),
        pl.BlockSpec((d, TF), lambda i, c: (0, c)),
        pl.BlockSpec((d, TF), lambda i, c: (0, c)),
        pl.BlockSpec((TF, d), lambda i, c: (c, 0)),
    ]
    args = [xs, g.reshape(1, d), mods, mods, mods, wg, wu, wd]
    if final_norm:
        in_specs.append(pl.BlockSpec((1, d), lambda i, c: (0, 0)))
        args.append(final_g.reshape(1, d))
        out_rows, aliases = n_tiles * TM, {}
    else:
        out_rows, aliases = m_tot, {0: 0}
    return pl.pallas_call(
        functools.partial(_ffn_kernel, final_norm=final_norm),
        out_shape=jax.ShapeDtypeStruct((out_rows, d), F32),
        grid=(n_tiles, ff // TF),
        in_specs=in_specs,
        out_specs=pl.BlockSpec((TM, d), lambda i, c: (i, 0)),
        scratch_shapes=[pltpu.VMEM((TM, d), BF16), pltpu.VMEM((TM, d), F32)],
        input_output_aliases=aliases,
        compiler_params=_cparams(("parallel", "arbitrary")),
        name="ffn_final" if final_norm else "ffn",
    )(*args)


def _proj_kernel(x_ref, g_ref, sh_ref, sc_ref, w_ref, o_ref, xn_ref):
    @pl.when(pl.program_id(1) == 0)
    def _():
        _norm_mod_store(xn_ref, x_ref, g_ref, sh_ref, sc_ref)

    o_ref[...] = jnp.dot(xn_ref[...], w_ref[...], preferred_element_type=F32).astype(o_ref.dtype)


def _proj(xs, mods, layer, g, w, out_dtype, row_of_tile, name):
    m_tot, d = xs.shape
    n = w.shape[1]
    return pl.pallas_call(
        _proj_kernel,
        out_shape=jax.ShapeDtypeStruct((m_tot, n), out_dtype),
        grid=(m_tot // TM, n // TN_PROJ),
        in_specs=[
            pl.BlockSpec((TM, d), lambda i, j: (i, 0)),
            pl.BlockSpec((1, d), lambda i, j: (0, 0)),
            _mod_spec(layer, 3, row_of_tile, d),
            _mod_spec(layer, 4, row_of_tile, d),
            pl.BlockSpec((d, TN_PROJ), lambda i, j: (0, j)),
        ],
        out_specs=pl.BlockSpec((TM, TN_PROJ), lambda i, j: (i, j)),
        scratch_shapes=[pltpu.VMEM((TM, d), BF16)],
        compiler_params=_cparams(("parallel", "arbitrary")),
        name=name,
    )(xs, g.reshape(1, d), mods, mods, w)


def _rope(t, cs, sn):
    lane = lax.broadcasted_iota(jnp.int32, t.shape, 1)
    first_half = (lane & (2 * ROPE_FREQS - 1)) < ROPE_FREQS
    partner = jnp.where(first_half,
                        pltpu.roll(t, HEAD_DIM - ROPE_FREQS, axis=1),
                        pltpu.roll(t, ROPE_FREQS, axis=1))
    return t * cs + partner * sn


def _softmax_pv(q4, kcat, vcat, sk, bias, scale):
    s = lax.dot_general(q4, kcat, (((1,), (1,)), ((), ())), preferred_element_type=F32) * scale
    if bias is not None:
        s = s + bias
    m = jnp.maximum(jnp.max(s, axis=-1, keepdims=True), sk)
    p = jnp.exp(s - m)
    denom = jnp.sum(p, axis=-1, keepdims=True) + jnp.exp(sk - m)
    o = jnp.dot(p.astype(BF16), vcat, preferred_element_type=F32)
    return o / denom


def _sink_rows(sink_ref, g, q_per_kv, rows):
    parts = [jnp.broadcast_to(sink_ref[g * q_per_kv + j:g * q_per_kv + j + 1, 0:1], (rows, 1))
             for j in range(q_per_kv)]
    return jnp.concatenate(parts, axis=0)


def _attn_lat_kernel(q_ref, kp_ref, kc_ref, kn_ref, vp_ref, vc_ref, vn_ref, kx_ref, vx_ref,
                     cos_ref, sin_ref, sink_ref, o_ref, *, nb, q_per_kv, scale):
    n = pl.program_id(1)
    blk = WINDOW_BLOCK
    dh = HEAD_DIM
    c_len = kx_ref.shape[0]

    def tables(block_idx):
        r0 = pl.multiple_of(block_idx * blk, blk)
        return cos_ref[pl.ds(r0, blk), :], sin_ref[pl.ds(r0, blk), :]

    cs_q, sn_q = tables(n)
    cs_p, sn_p = tables(jnp.maximum(n - 1, 0))
    cs_n, sn_n = tables(jnp.minimum(n + 1, nb - 1))

    rows = q_per_kv * blk
    keys = 3 * blk + c_len
    ri = lax.broadcasted_iota(jnp.int32, (rows, keys), 0) & (blk - 1)
    kj = lax.broadcasted_iota(jnp.int32, (rows, keys), 1)
    lo = jnp.where(n == 0, blk, 0)
    hi = jnp.where(n == nb - 1, 2 * blk, 3 * blk)
    valid = (kj >= 3 * blk) | ((kj >= ri) & (kj <= ri + 2 * blk) & (kj >= lo) & (kj < hi))
    bias = jnp.where(valid, 0.0, NEG_BIG).astype(F32)

    for g in range(N_KV_HEADS):
        cols = slice(g * dh, (g + 1) * dh)
        kcat = jnp.concatenate([
            _rope(kp_ref[:, cols].astype(F32), cs_p, sn_p).astype(BF16),
            _rope(kc_ref[:, cols].astype(F32), cs_q, sn_q).astype(BF16),
            _rope(kn_ref[:, cols].astype(F32), cs_n, sn_n).astype(BF16),
            kx_ref[:, cols]], axis=0)
        vcat = jnp.concatenate([vp_ref[:, cols], vc_ref[:, cols], vn_ref[:, cols], vx_ref[:, cols]],
                               axis=0)
        q4 = jnp.concatenate([
            _rope(q_ref[:, (g * q_per_kv + j) * dh:(g * q_per_kv + j + 1) * dh].astype(F32),
                  cs_q, sn_q).astype(BF16) for j in range(q_per_kv)], axis=0)
        sk = _sink_rows(sink_ref, g, q_per_kv, blk)
        o = _softmax_pv(q4, kcat, vcat, sk, bias, scale)
        for j in range(q_per_kv):
            h = g * q_per_kv + j
            o_ref[:, h * dh:(h + 1) * dh] = o[j * blk:(j + 1) * blk].astype(o_ref.dtype)


def _attn_ctx_kernel(q_ref, kx_ref, vx_ref, sink_ref, ya_ref, o_ref, *, q_per_kv, scale):
    del ya_ref
    dh = HEAD_DIM
    c_len = q_ref.shape[0]
    for g in range(N_KV_HEADS):
        cols = slice(g * dh, (g + 1) * dh)
        q4 = jnp.concatenate([q_ref[:, (g * q_per_kv + j) * dh:(g * q_per_kv + j + 1) * dh]
                              for j in range(q_per_kv)], axis=0)
        sk = _sink_rows(sink_ref, g, q_per_kv, c_len)
        o = _softmax_pv(q4, kx_ref[:, cols], vx_ref[:, cols], sk, None, scale)
        for j in range(q_per_kv):
            h = g * q_per_kv + j
            o_ref[:, h * dh:(h + 1) * dh] = o[j * c_len:(j + 1) * c_len].astype(o_ref.dtype)


def _attention(qkv, cos_t, sin_t, sink_b, bsz, s_len, c_len, attn_w, need_ctx):
    m_tot = qkv.shape[0]
    kv_w = N_KV_HEADS * HEAD_DIM
    n_q_heads = attn_w // HEAD_DIM
    q_per_kv = n_q_heads // N_KV_HEADS
    blk = WINDOW_BLOCK
    nb = s_len // blk
    k_col = attn_w // kv_w
    v_col = k_col + 1
    ctx_blk0 = (bsz * s_len) // c_len
    scale = HEAD_DIM ** -0.5

    def kv_spec(col, shift):
        def idx(b, n):
            return (b * nb + jnp.clip(n + shift, 0, nb - 1), col)
        return pl.BlockSpec((blk, kv_w), idx)

    ctx_k = pl.BlockSpec((c_len, kv_w), lambda b, n: (ctx_blk0 + b, k_col))
    ctx_v = pl.BlockSpec((c_len, kv_w), lambda b, n: (ctx_blk0 + b, v_col))
    ya = pl.pallas_call(
        functools.partial(_attn_lat_kernel, nb=nb, q_per_kv=q_per_kv, scale=scale),
        out_shape=jax.ShapeDtypeStruct((m_tot, attn_w), BF16),
        grid=(bsz, nb),
        in_specs=[
            pl.BlockSpec((blk, attn_w), lambda b, n: (b * nb + n, 0)),
            kv_spec(k_col, -1), kv_spec(k_col, 0), kv_spec(k_col, 1),
            kv_spec(v_col, -1), kv_spec(v_col, 0), kv_spec(v_col, 1),
            ctx_k, ctx_v,
            pl.BlockSpec((s_len, HEAD_DIM), lambda b, n: (0, 0)),
            pl.BlockSpec((s_len, HEAD_DIM), lambda b, n: (0, 0)),
            pl.BlockSpec((n_q_heads, LANES), lambda b, n: (0, 0)),
        ],
        out_specs=pl.BlockSpec((blk, attn_w), lambda b, n: (b * nb + n, 0)),
        compiler_params=_cparams(("parallel", "parallel")),
        name="attn_latent",
    )(qkv, qkv, qkv, qkv, qkv, qkv, qkv, qkv, qkv, cos_t, sin_t, sink_b)
    if not need_ctx:
        return ya
    return pl.pallas_call(
        functools.partial(_attn_ctx_kernel, q_per_kv=q_per_kv, scale=scale),
        out_shape=jax.ShapeDtypeStruct((m_tot, attn_w), BF16),
        grid=(bsz,),
        in_specs=[
            pl.BlockSpec((c_len, attn_w), lambda b: (ctx_blk0 + b, 0)),
            pl.BlockSpec((c_len, kv_w), lambda b: (ctx_blk0 + b, k_col)),
            pl.BlockSpec((c_len, kv_w), lambda b: (ctx_blk0 + b, v_col)),
            pl.BlockSpec((n_q_heads, LANES), lambda b: (0, 0)),
            pl.BlockSpec(memory_space=pl.ANY),
        ],
        out_specs=pl.BlockSpec((c_len, attn_w), lambda b: (ctx_blk0 + b, 0)),
        input_output_aliases={4: 0},
        compiler_params=_cparams(("parallel",)),
        name="attn_context",
    )(qkv, qkv, qkv, sink_b, ya)


def _scan8(a, b, reverse):
    rows = a.shape[0]
    sub = lax.broadcasted_iota(jnp.int32, a.shape, 0) & (SUBLANES - 1)
    s = 1
    while s < SUBLANES:
        if reverse:
            shift, take = rows - s, sub < SUBLANES - s
        else:
            shift, take = s, sub >= s
        a_sh = pltpu.roll(a, shift, axis=0)
        b_sh = pltpu.roll(b, shift, axis=0)
        b = jnp.where(take, a * b_sh + b, b)
        a = jnp.where(take, a * a_sh, a)
        s *= 2
    return a, b


def _rglru_kernel(x_ref, cw_ref, cb_ref, w_ref, b_ref, lam_ref, h0_ref, *refs, seq, aliased):
    if aliased:
        _, y_ref, hend_ref, xp_ref = refs
    else:
        y_ref, hend_ref, xp_ref = refs
    rc = min(SCAN_ROWS, seq)
    n_chunks = seq // rc
    groups = rc // SUBLANES
    pad = SUBLANES
    width = x_ref.shape[1]

    xp_ref[0:pad, :] = jnp.zeros((pad, width), F32)
    xp_ref[pad:pad + seq, :] = x_ref[...]
    xp_ref[pad + seq:pad + seq + pad, :] = jnp.zeros((pad, width), F32)

    cw = cw_ref[...]
    cb = cb_ref[...]

    def chunk(d, r0, hb):
        reverse = d == 1
        win = xp_ref[pl.ds(r0, rc + 2 * pad), :]
        u = cb
        for k in range(cw.shape[0]):
            off = pad - CONV_LEFT + k
            u = u + cw[k:k + 1, :] * win[off:off + rc, :]
        lam = lam_ref[d:d + 1, :]
        neg = -lam
        softplus = jnp.maximum(neg, 0.0) + jnp.log1p(jnp.exp(-jnp.abs(neg)))
        gr = jax.nn.sigmoid(jnp.dot(u, w_ref[d, 0], preferred_element_type=F32,
                                    precision=lax.Precision.HIGHEST) + b_ref[d, 0:1, :])
        gi = jax.nn.sigmoid(jnp.dot(u, w_ref[d, 1], preferred_element_type=F32,
                                    precision=lax.Precision.HIGHEST) + b_ref[d, 1:2, :])
        log_a = (-RGLRU_C * gr) * softplus
        a = jnp.exp(log_a)
        bx = jnp.sqrt(-jnp.tanh(log_a) * (a * a + 1.0)) * (gi * u)
        a_cum, b_cum = _scan8(a, bx, reverse)
        outs = [None] * groups
        order = range(groups - 1, -1, -1) if reverse else range(groups)
        for k in order:
            h = a_cum[k * SUBLANES:(k + 1) * SUBLANES] * hb + b_cum[k * SUBLANES:(k + 1) * SUBLANES]
            outs[k] = h
            edge = h[0:1, :] if reverse else h[SUBLANES - 1:SUBLANES, :]
            hb = jnp.broadcast_to(edge, (SUBLANES, width))
        hs = jnp.concatenate(outs, axis=0)
        if reverse:
            y_ref[pl.ds(r0, rc), :] += hs
        else:
            y_ref[pl.ds(r0, rc), :] = hs
        return hb

    for d in range(2):
        def body(i, hb, d=d):
            ci = (n_chunks - 1 - i) if d == 1 else i
            return chunk(d, pl.multiple_of(ci * rc, rc), hb)

        hb0 = jnp.broadcast_to(h0_ref[d:d + 1, :], (SUBLANES, width))
        hb = lax.fori_loop(0, n_chunks, body, hb0)
        hend_ref[d:d + 1, :] = hb[0:1, :]


def _rglru(rest, conv_w, conv_b, rg_w, rg_b, rg_lambda, h0, r_prev, seq, row_blk0, bsz, m_tot):
    rnn_w = conv_w.shape[-1]
    n_cb = rg_w.shape[2]
    cbw = rnn_w // n_cb
    aliased = r_prev is not None
    in_specs = [
        pl.BlockSpec((seq, cbw), lambda b, j: (row_blk0 + b, j)),
        pl.BlockSpec((conv_w.shape[0], cbw), lambda b, j: (0, j)),
        pl.BlockSpec((1, cbw), lambda b, j: (0, j)),
        pl.BlockSpec((2, 2, None, cbw, cbw), lambda b, j: (0, 0, j, 0, 0)),
        pl.BlockSpec((2, 2, cbw), lambda b, j: (0, 0, j)),
        pl.BlockSpec((2, cbw), lambda b, j: (0, j)),
        pl.BlockSpec((None, 2, cbw), lambda b, j: (b, 0, j)),
    ]
    args = [rest, conv_w, conv_b.reshape(1, rnn_w), rg_w, rg_b, rg_lambda, h0]
    aliases = {}
    if aliased:
        in_specs.append(pl.BlockSpec(memory_space=pl.ANY))
        args.append(r_prev)
        aliases = {len(args) - 1: 0}
    return pl.pallas_call(
        functools.partial(_rglru_kernel, seq=seq, aliased=aliased),
        out_shape=(jax.ShapeDtypeStruct((m_tot, rnn_w), F32),
                   jax.ShapeDtypeStruct((bsz, 2, rnn_w), F32)),
        grid=(bsz, n_cb),
        in_specs=in_specs,
        out_specs=(pl.BlockSpec((seq, cbw), lambda b, j: (row_blk0 + b, j)),
                   pl.BlockSpec((None, 2, cbw), lambda b, j: (b, 0, j))),
        scratch_shapes=[pltpu.VMEM((seq + 2 * SUBLANES, cbw), F32)],
        input_output_aliases=aliases,
        compiler_params=_cparams(("parallel", "parallel")),
        name="rglru_latent" if aliased else "rglru_context",
    )(*args)


def _gelu_tanh(x):
    return 0.5 * x * (1.0 + jnp.tanh(0.7978845608028654 * (x + 0.044715 * (x * x * x))))


def _merge_kernel(x_ref, ya_ref, r_ref, gr_ref, ga_ref, gb_ref, gt_ref, woa_ref, wor_ref, wout_ref,
                  o_ref, yr_ref, acc_ref):
    c = pl.program_id(1)

    @pl.when(c == 0)
    def _():
        yr_ref[...] = (r_ref[...] * _gelu_tanh(gr_ref[...])).astype(yr_ref.dtype)
        acc_ref[...] = jnp.zeros_like(acc_ref)

    pa = jnp.dot(ya_ref[...], woa_ref[...], preferred_element_type=F32)
    pr = jnp.dot(yr_ref[...], wor_ref[...], preferred_element_type=F32)
    t = jax.nn.sigmoid(ga_ref[...]) * pa + jax.nn.sigmoid(gb_ref[...]) * pr
    acc_ref[...] += jnp.dot(t.astype(BF16), wout_ref[...], preferred_element_type=F32)

    @pl.when(c == pl.num_programs(1) - 1)
    def _():
        o_ref[...] = x_ref[...] + gt_ref[...] * acc_ref[...]


def _merge(xs, ya, r, rest, mods, layer, woa, wor, wout, n_tiles, row_of_tile):
    m_tot, d = xs.shape
    attn_w = ya.shape[1]
    rnn_w = r.shape[1]
    ga_blk0 = (2 * rnn_w) // TF
    gb_blk0 = (2 * rnn_w + d) // TF
    return pl.pallas_call(
        _merge_kernel,
        out_shape=jax.ShapeDtypeStruct((m_tot, d), F32),
        grid=(n_tiles, d // TF),
        in_specs=[
            pl.BlockSpec((TM, d), lambda i, c: (i, 0)),
            pl.BlockSpec((TM, attn_w), lambda i, c: (i, 0)),
            pl.BlockSpec((TM, rnn_w), lambda i, c: (i, 0)),
            pl.BlockSpec((TM, rnn_w), lambda i, c: (i, 1)),
            pl.BlockSpec((TM, TF), lambda i, c: (i, ga_blk0 + c)),
            pl.BlockSpec((TM, TF), lambda i, c: (i, gb_blk0 + c)),
            _mod_spec(layer, 5, row_of_tile, d),
            pl.BlockSpec((attn_w, TF), lambda i, c: (0, c)),
            pl.BlockSpec((rnn_w, TF), lambda i, c: (0, c)),
            pl.BlockSpec((TF, d), lambda i, c: (c, 0)),
        ],
        out_specs=pl.BlockSpec((TM, d), lambda i, c: (i, 0)),
        scratch_shapes=[pltpu.VMEM((TM, rnn_w), BF16), pltpu.VMEM((TM, d), F32)],
        input_output_aliases={0: 0},
        compiler_params=_cparams(("parallel", "arbitrary")),
        name="merge",
    )(xs, ya, r, rest, rest, rest, mods, woa, wor, wout)


def _rope_tables(s_len):
    pos = jnp.arange(s_len, dtype=jnp.int32)
    row = (pos // GRID_W).astype(F32)
    col = (pos % GRID_W).astype(F32)
    inv_freq = ROPE_BASE ** (-jnp.arange(ROPE_FREQS, dtype=F32) / ROPE_FREQS)
    ang_r = row[:, None] * inv_freq
    ang_c = col[:, None] * inv_freq
    cos_t = jnp.concatenate([jnp.cos(ang_r), jnp.cos(ang_r), jnp.cos(ang_c), jnp.cos(ang_c)], axis=1)
    sin_t = jnp.concatenate([-jnp.sin(ang_r), jnp.sin(ang_r), -jnp.sin(ang_c), jnp.sin(ang_c)], axis=1)
    return cos_t, sin_t


def _pad_cols(w, n):
    return jnp.pad(w, ((0, 0), (0, n - w.shape[1])))


def kernel(x, c, ctx, c_ctx, w_ada, b_ada, norm_g, final_g, w_ffn_up, w_ffn_down, w_in, attn_sink,
           conv_w, conv_b, rg_w, rg_b, rg_lambda, w_o_attn, w_o_rnn, w_out):
    bsz, s_len, d = x.shape
    c_len = ctx.shape[1]
    depth = w_ada.shape[0]
    d_ff = w_ffn_down.shape[2]
    attn_w = w_o_attn.shape[1]
    rnn_w = w_o_rnn.shape[1]
    kv_w = N_KV_HEADS * HEAD_DIM
    qkv_w = attn_w + 2 * kv_w
    ff_pad = -(-d_ff // TF) * TF
    n_lat = bsz * s_len
    m_tot = n_lat + bsz * c_len
    assert s_len % TM == 0 and (bsz * c_len) % TM == 0 and c_len % SUBLANES == 0
    assert bsz + 1 <= MOD_ROWS and attn_w % kv_w == 0 and n_lat % c_len == 0
    n_lat_tiles = n_lat // TM
    n_all_tiles = m_tot // TM
    row_of_tile = _row_of_tile(n_lat_tiles, s_len // TM, bsz)

    c_all = jnp.zeros((MOD_ROWS, d), F32).at[:bsz].set(c).at[bsz].set(c_ctx)
    mods = _adaln(c_all, w_ada, b_ada).reshape(depth, MOD_ROWS, 1, N_MOD * d)
    cos_t, sin_t = _rope_tables(s_len)
    xs = jnp.concatenate([x.reshape(n_lat, d), ctx.reshape(bsz * c_len, d)], axis=0)
    h_zero = jnp.zeros((bsz, 2, rnn_w), F32)

    for l in range(depth):
        need_ctx = l < depth - 1
        last = l == depth - 1
        n_tiles = n_all_tiles if need_ctx else n_lat_tiles

        def ffn_weights(s):
            up = w_ffn_up[l, s]
            wg = _pad_cols(up[:, :d_ff], ff_pad).astype(BF16)
            wu = _pad_cols(up[:, d_ff:], ff_pad).astype(BF16)
            wd = jnp.pad(w_ffn_down[l, s], ((0, ff_pad - d_ff), (0, 0))).astype(BF16)
            return wg, wu, wd

        xs = _ffn(xs, mods, l, 0, norm_g[l, 0], *ffn_weights(0), n_all_tiles, row_of_tile)

        w_qkv = w_in[l, :, :qkv_w].astype(BF16)
        w_rest = w_in[l, :, qkv_w:].astype(BF16)
        qkv = _proj(xs, mods, l, norm_g[l, 1], w_qkv, BF16, row_of_tile, "proj_qkv")
        rest = _proj(xs, mods, l, norm_g[l, 1], w_rest, F32, row_of_tile, "proj_rest")

        sink_b = jnp.broadcast_to(attn_sink[l][:, None], (attn_sink.shape[1], LANES))
        ya = _attention(qkv, cos_t, sin_t, sink_b, bsz, s_len, c_len, attn_w, need_ctx)

        r_ctx, h_end = _rglru(rest, conv_w[l], conv_b[l], rg_w[l], rg_b[l], rg_lambda[l], h_zero,
                              None, c_len, n_lat // c_len, bsz, m_tot)
        r_all, _ = _rglru(rest, conv_w[l], conv_b[l], rg_w[l], rg_b[l], rg_lambda[l], h_end,
                          r_ctx, s_len, 0, bsz, m_tot)

        xs = _merge(xs, ya, r_all, rest, mods, l, w_o_attn[l].astype(BF16), w_o_rnn[l].astype(BF16),
                    w_out[l].astype(BF16), n_tiles, row_of_tile)
        xs = _ffn(xs, mods, l, 6, norm_g[l, 2], *ffn_weights(1), n_tiles, row_of_tile,
                  final_g=final_g if last else None)

    return xs.reshape(bsz, s_len, d)
```

```python
import functools

import jax
import jax.numpy as jnp
from jax import lax
from jax.experimental import pallas as pl
from jax.experimental.pallas import tpu as pltpu

F32 = jnp.float32
BF16 = jnp.bfloat16

LANES = 128
SUBLANES = 8
VMEM_PHYSICAL_BYTES = 64 * 1024 * 1024
VMEM_HEADROOM_BYTES = 4 * 1024 * 1024

HEAD_DIM = 128
N_KV_HEADS = 2
WINDOW_BLOCK = 128
GRID_W = 64
ROPE_BASE = 10000.0
ROPE_FREQS = HEAD_DIM // 4
CONV_LEFT = 2
RGLRU_C = 8.0
N_MOD = 9
EPS = 1e-6
MOD_ROWS = 16
NEG_BIG = -1e30

TM_FFN = 1024
TF = 512
TN_PROJ = 1536
TM_MERGE = 512
TC_MERGE = 1024
TN_ADA = 1024
NORM_ROWS = 128
SCAN_CHUNK = 512
SCAN_SUB = 32
SCAN_WIDTH = 256


def _cparams(sem, block_bytes):
    limit = min(int(block_bytes) + VMEM_HEADROOM_BYTES, VMEM_PHYSICAL_BYTES - VMEM_HEADROOM_BYTES // 2)
    return pltpu.CompilerParams(dimension_semantics=sem, vmem_limit_bytes=limit)


def _adaln_kernel(c_ref, w_ref, b_ref, o_ref):
    c = c_ref[...]
    sc = c * jax.nn.sigmoid(c)
    o_ref[...] = jnp.dot(sc, w_ref[...], preferred_element_type=F32,
                         precision=lax.Precision.HIGHEST) + b_ref[...]


def _adaln(c_all, w_ada, b_ada):
    depth, d, n = w_ada.shape
    return pl.pallas_call(
        _adaln_kernel,
        out_shape=jax.ShapeDtypeStruct((depth, MOD_ROWS, n), F32),
        grid=(depth, n // TN_ADA),
        in_specs=[
            pl.BlockSpec((MOD_ROWS, d), lambda l, j: (0, 0)),
            pl.BlockSpec((None, d, TN_ADA), lambda l, j: (l, 0, j)),
            pl.BlockSpec((None, 1, TN_ADA), lambda l, j: (l, 0, j)),
        ],
        out_specs=pl.BlockSpec((None, MOD_ROWS, TN_ADA), lambda l, j: (l, 0, j)),
        compiler_params=_cparams(("parallel", "parallel"), 2 * d * TN_ADA * 4 + 4 * MOD_ROWS * (d + TN_ADA) * 4),
        name="adaln",
    )(c_all, w_ada, b_ada.reshape(depth, 1, n))


def _rms_scale(x):
    return x * lax.rsqrt(jnp.mean(x * x, axis=-1, keepdims=True) + EPS)


def _norm_mod_store(xn_ref, x_ref, g_ref, sh_ref, sc_ref, zero_ref=None):
    g = g_ref[...]
    sh = sh_ref[...]
    sc1 = 1.0 + sc_ref[...]

    def body(i, carry):
        r0 = pl.multiple_of(i * NORM_ROWS, NORM_ROWS)
        y = _rms_scale(x_ref[pl.ds(r0, NORM_ROWS), :]) * g
        xn_ref[pl.ds(r0, NORM_ROWS), :] = (y * sc1 + sh).astype(xn_ref.dtype)
        if zero_ref is not None:
            zero_ref[pl.ds(r0, NORM_ROWS), :] = jnp.zeros((NORM_ROWS, zero_ref.shape[1]), zero_ref.dtype)
        return carry

    lax.fori_loop(0, x_ref.shape[0] // NORM_ROWS, body, 0)


def _mod_spec(layer, k, row_of_tile, d):
    return pl.BlockSpec((None, None, 1, d), lambda i, j: (layer, row_of_tile(i), 0, k))


def _ffn_kernel(x_ref, g_ref, sh_ref, sc_ref, gt_ref, wg_ref, wu_ref, wd_ref, *refs, final_norm):
    if final_norm:
        fg_ref, o_ref, xn_ref = refs
    else:
        o_ref, xn_ref = refs
    c = pl.program_id(1)

    @pl.when(c == 0)
    def _():
        _norm_mod_store(xn_ref, x_ref, g_ref, sh_ref, sc_ref, zero_ref=o_ref)

    xn = xn_ref[...]
    h = jnp.dot(xn, wg_ref[...], preferred_element_type=F32)
    u = jnp.dot(xn, wu_ref[...], preferred_element_type=F32)
    a = ((h * jax.nn.sigmoid(h)) * u).astype(BF16)
    o_ref[...] += jnp.dot(a, wd_ref[...], preferred_element_type=F32)

    @pl.when(c == pl.num_programs(1) - 1)
    def _():
        gate = 0.5 * gt_ref[...]

        def body(i, carry):
            r0 = pl.multiple_of(i * NORM_ROWS, NORM_ROWS)
            y = x_ref[pl.ds(r0, NORM_ROWS), :] + gate * o_ref[pl.ds(r0, NORM_ROWS), :]
            if final_norm:
                y = _rms_scale(y) * fg_ref[...]
            o_ref[pl.ds(r0, NORM_ROWS), :] = y
            return carry

        lax.fori_loop(0, x_ref.shape[0] // NORM_ROWS, body, 0)


def _ffn(xs, mods, layer, mod_k0, g, wg, wu, wd, row_of_tile, final_g=None):
    rows, d = xs.shape
    ff = wg.shape[1]
    tm = TM_FFN
    final_norm = final_g is not None
    in_specs = [
        pl.BlockSpec((tm, d), lambda i, c: (i, 0)),
        pl.BlockSpec((1, d), lambda i, c: (0, 0)),
        _mod_spec(layer, mod_k0, row_of_tile, d),
        _mod_spec(layer, mod_k0 + 1, row_of_tile, d),
        _mod_spec(layer, mod_k0 + 2, row_of_tile, d),
        pl.BlockSpec((d, TF), lambda i, c: (0, c)),
        pl.BlockSpec((d, TF), lambda i, c: (0, c)),
        pl.BlockSpec((TF, d), lambda i, c: (c, 0)),
    ]
    args = [xs, g.reshape(1, d), mods, mods, mods, wg, wu, wd]
    if final_norm:
        in_specs.append(pl.BlockSpec((1, d), lambda i, c: (0, 0)))
        args.append(final_g.reshape(1, d))
    block_bytes = (2 * tm * d * 4 + 2 * tm * d * 4 + 2 * 3 * d * TF * 2 + tm * d * 2
                   + 2 * tm * TF * 4 + tm * TF * 2)
    return pl.pallas_call(
        functools.partial(_ffn_kernel, final_norm=final_norm),
        out_shape=jax.ShapeDtypeStruct((rows, d), F32),
        grid=(rows // tm, ff // TF),
        in_specs=in_specs,
        out_specs=pl.BlockSpec((tm, d), lambda i, c: (i, 0)),
        scratch_shapes=[pltpu.VMEM((tm, d), BF16)],
        input_output_aliases={} if final_norm else {0: 0},
        compiler_params=_cparams(("parallel", "arbitrary"), block_bytes),
        name="ffn_final" if final_norm else "ffn",
    )(*args)


def _proj_kernel(x_ref, g_ref, sh_ref, sc_ref, w_ref, o_ref, xn_ref):
    @pl.when(pl.program_id(1) == 0)
    def _():
        _norm_mod_store(xn_ref, x_ref, g_ref, sh_ref, sc_ref)

    o_ref[...] = jnp.dot(xn_ref[...], w_ref[...], preferred_element_type=F32).astype(o_ref.dtype)


def _proj(xs, mods, layer, g, w, row_of_tile):
    rows, d = xs.shape
    n = w.shape[1]
    tm = TM_FFN
    block_bytes = 2 * tm * d * 4 + tm * d * 2 + 2 * d * TN_PROJ * 2 + 2 * tm * TN_PROJ * 2 + tm * TN_PROJ * 4
    return pl.pallas_call(
        _proj_kernel,
        out_shape=jax.ShapeDtypeStruct((rows, n), BF16),
        grid=(rows // tm, n // TN_PROJ),
        in_specs=[
            pl.BlockSpec((tm, d), lambda i, j: (i, 0)),
            pl.BlockSpec((1, d), lambda i, j: (0, 0)),
            _mod_spec(layer, 3, row_of_tile, d),
            _mod_spec(layer, 4, row_of_tile, d),
            pl.BlockSpec((d, TN_PROJ), lambda i, j: (0, j)),
        ],
        out_specs=pl.BlockSpec((tm, TN_PROJ), lambda i, j: (i, j)),
        scratch_shapes=[pltpu.VMEM((tm, d), BF16)],
        compiler_params=_cparams(("parallel", "arbitrary"), block_bytes),
        name="in_proj",
    )(xs, g.reshape(1, d), mods, mods, w)


def _rope(t, cs, sn):
    lane = lax.broadcasted_iota(jnp.int32, t.shape, 1)
    first_half = (lane & (2 * ROPE_FREQS - 1)) < ROPE_FREQS
    partner = jnp.where(first_half,
                        pltpu.roll(t, HEAD_DIM - ROPE_FREQS, axis=1),
                        pltpu.roll(t, ROPE_FREQS, axis=1))
    return t * cs + partner * sn


def _softmax_pv(q4, kcat, vcat, sk, bias, scale):
    s = lax.dot_general(q4, kcat, (((1,), (1,)), ((), ())), preferred_element_type=F32) * scale
    if bias is not None:
        s = s + bias
    m = jnp.maximum(jnp.max(s, axis=-1, keepdims=True), sk)
    p = jnp.exp(s - m)
    denom = jnp.sum(p, axis=-1, keepdims=True) + jnp.exp(sk - m)
    o = jnp.dot(p.astype(BF16), vcat, preferred_element_type=F32)
    return o / denom


def _sink_rows(sink_ref, g, q_per_kv, rows):
    parts = [jnp.broadcast_to(sink_ref[g * q_per_kv + j:g * q_per_kv + j + 1, 0:1], (rows, 1))
             for j in range(q_per_kv)]
    return jnp.concatenate(parts, axis=0)


def _attn_lat_kernel(q_ref, kp_ref, kc_ref, kn_ref, vp_ref, vc_ref, vn_ref, kx_ref, vx_ref,
                     cos_ref, sin_ref, sink_ref, o_ref, *, nb, q_per_kv, scale):
    n = pl.program_id(1)
    blk = WINDOW_BLOCK
    dh = HEAD_DIM
    c_len = kx_ref.shape[0]

    def tables(block_idx):
        r0 = pl.multiple_of(block_idx * blk, blk)
        return cos_ref[pl.ds(r0, blk), :], sin_ref[pl.ds(r0, blk), :]

    cs_q, sn_q = tables(n)
    cs_p, sn_p = tables(jnp.maximum(n - 1, 0))
    cs_n, sn_n = tables(jnp.minimum(n + 1, nb - 1))

    rows = q_per_kv * blk
    keys = 3 * blk + c_len
    ri = lax.broadcasted_iota(jnp.int32, (rows, keys), 0) & (blk - 1)
    kj = lax.broadcasted_iota(jnp.int32, (rows, keys), 1)
    lo = jnp.where(n == 0, blk, 0)
    hi = jnp.where(n == nb - 1, 2 * blk, 3 * blk)
    valid = (kj >= 3 * blk) | ((kj >= ri) & (kj <= ri + 2 * blk) & (kj >= lo) & (kj < hi))
    bias = jnp.where(valid, 0.0, NEG_BIG).astype(F32)

    for g in range(N_KV_HEADS):
        cols = slice(g * dh, (g + 1) * dh)
        kcat = jnp.concatenate([
            _rope(kp_ref[:, cols].astype(F32), cs_p, sn_p).astype(BF16),
            _rope(kc_ref[:, cols].astype(F32), cs_q, sn_q).astype(BF16),
            _rope(kn_ref[:, cols].astype(F32), cs_n, sn_n).astype(BF16),
            kx_ref[:, cols]], axis=0)
        vcat = jnp.concatenate([vp_ref[:, cols], vc_ref[:, cols], vn_ref[:, cols], vx_ref[:, cols]],
                               axis=0)
        q4 = jnp.concatenate([
            _rope(q_ref[:, (g * q_per_kv + j) * dh:(g * q_per_kv + j + 1) * dh].astype(F32),
                  cs_q, sn_q).astype(BF16) for j in range(q_per_kv)], axis=0)
        sk = _sink_rows(sink_ref, g, q_per_kv, blk)
        o = _softmax_pv(q4, kcat, vcat, sk, bias, scale)
        for j in range(q_per_kv):
            h = g * q_per_kv + j
            o_ref[:, h * dh:(h + 1) * dh] = o[j * blk:(j + 1) * blk].astype(o_ref.dtype)


def _attn_ctx_kernel(q_ref, kx_ref, vx_ref, sink_ref, o_ref, *, q_per_kv, scale):
    dh = HEAD_DIM
    c_len = q_ref.shape[0]
    for g in range(N_KV_HEADS):
        cols = slice(g * dh, (g + 1) * dh)
        q4 = jnp.concatenate([q_ref[:, (g * q_per_kv + j) * dh:(g * q_per_kv + j + 1) * dh]
                              for j in range(q_per_kv)], axis=0)
        sk = _sink_rows(sink_ref, g, q_per_kv, c_len)
        o = _softmax_pv(q4, kx_ref[:, cols], vx_ref[:, cols], sk, None, scale)
        for j in range(q_per_kv):
            h = g * q_per_kv + j
            o_ref[:, h * dh:(h + 1) * dh] = o[j * c_len:(j + 1) * c_len].astype(o_ref.dtype)


def _attention(p_lat, p_ctx, cos_t, sin_t, sink_b, bsz, s_len, c_len, attn_w, k_off, need_ctx):
    kv_w = N_KV_HEADS * HEAD_DIM
    n_q_heads = attn_w // HEAD_DIM
    q_per_kv = n_q_heads // N_KV_HEADS
    blk = WINDOW_BLOCK
    nb = s_len // blk
    k_col = k_off // kv_w
    v_col = k_col + 1
    scale = HEAD_DIM ** -0.5

    def kv_spec(col, shift):
        def idx(b, n):
            return (b * nb + jnp.clip(n + shift, 0, nb - 1), col)
        return pl.BlockSpec((blk, kv_w), idx)

    ya_lat = pl.pallas_call(
        functools.partial(_attn_lat_kernel, nb=nb, q_per_kv=q_per_kv, scale=scale),
        out_shape=jax.ShapeDtypeStruct((bsz * s_len, attn_w), BF16),
        grid=(bsz, nb),
        in_specs=[
            pl.BlockSpec((blk, attn_w), lambda b, n: (b * nb + n, 0)),
            kv_spec(k_col, -1), kv_spec(k_col, 0), kv_spec(k_col, 1),
            kv_spec(v_col, -1), kv_spec(v_col, 0), kv_spec(v_col, 1),
            pl.BlockSpec((c_len, kv_w), lambda b, n: (b, k_col)),
            pl.BlockSpec((c_len, kv_w), lambda b, n: (b, v_col)),
            pl.BlockSpec((s_len, HEAD_DIM), lambda b, n: (0, 0)),
            pl.BlockSpec((s_len, HEAD_DIM), lambda b, n: (0, 0)),
            pl.BlockSpec((n_q_heads, LANES), lambda b, n: (0, 0)),
        ],
        out_specs=pl.BlockSpec((blk, attn_w), lambda b, n: (b * nb + n, 0)),
        compiler_params=_cparams(("parallel", "parallel"), 4 * s_len * HEAD_DIM * 4 + 16 * 1024 * 1024),
        name="attn_latent",
    )(p_lat, p_lat, p_lat, p_lat, p_lat, p_lat, p_lat, p_ctx, p_ctx, cos_t, sin_t, sink_b)
    if not need_ctx:
        return ya_lat, None
    ya_ctx = pl.pallas_call(
        functools.partial(_attn_ctx_kernel, q_per_kv=q_per_kv, scale=scale),
        out_shape=jax.ShapeDtypeStruct((bsz * c_len, attn_w), BF16),
        grid=(bsz,),
        in_specs=[
            pl.BlockSpec((c_len, attn_w), lambda b: (b, 0)),
            pl.BlockSpec((c_len, kv_w), lambda b: (b, k_col)),
            pl.BlockSpec((c_len, kv_w), lambda b: (b, v_col)),
            pl.BlockSpec((n_q_heads, LANES), lambda b: (0, 0)),
        ],
        out_specs=pl.BlockSpec((c_len, attn_w), lambda b: (b, 0)),
        compiler_params=_cparams(("parallel",), 16 * 1024 * 1024),
        name="attn_context",
    )(p_ctx, p_ctx, p_ctx, sink_b)
    return ya_lat, ya_ctx


def _rglru_kernel(xp_ref, x_ref, xn_ref, cw_ref, cb_ref, w_ref, b_ref, lam_ref, h0_ref, *refs,
                  reverse, nk):
    if reverse:
        hf_ref, y_ref, hend_ref, xt_ref, hs_ref, h_ref = refs
    else:
        y_ref, hend_ref, xt_ref, hs_ref, h_ref = refs
    k = pl.program_id(1)
    ci = nk - 1 - k if reverse else k
    bsz, tc, width = x_ref.shape
    n_lb = width // LANES
    halo = xp_ref.shape[1]
    n_sub = tc // SCAN_SUB
    rows_sub = SCAN_SUB * bsz

    @pl.when(k == 0)
    def _():
        h_ref[...] = h0_ref[...]

    for c in range(n_lb):
        lanes = slice(c * LANES, (c + 1) * LANES)
        for b in range(bsz):
            xt_ref[c, pl.ds(b, halo, stride=bsz), :] = xp_ref[b, :, lanes].astype(F32)
            xt_ref[c, pl.ds(halo * bsz + b, tc, stride=bsz), :] = x_ref[b, :, lanes].astype(F32)
            xt_ref[c, pl.ds((halo + tc) * bsz + b, halo, stride=bsz), :] = xn_ref[b, :, lanes].astype(F32)

    @pl.when(ci == 0)
    def _():
        xt_ref[:, 0:halo * bsz, :] = jnp.zeros((n_lb, halo * bsz, LANES), F32)

    @pl.when(ci == nk - 1)
    def _():
        xt_ref[:, (halo + tc) * bsz:(2 * halo + tc) * bsz, :] = jnp.zeros((n_lb, halo * bsz, LANES), F32)

    def sub_body(i, hs):
        si = (n_sub - 1 - i) if reverse else i
        t0 = si * SCAN_SUB
        new_h = []
        for c in range(n_lb):
            lanes = slice(c * LANES, (c + 1) * LANES)
            cw = cw_ref[:, lanes]
            u = cb_ref[:, lanes]
            for kk in range(cw.shape[0]):
                r0 = pl.multiple_of((t0 + halo - CONV_LEFT + kk) * bsz, bsz)
                u = u + cw[kk:kk + 1, :] * xt_ref[c, pl.ds(r0, rows_sub), :]
            g = jnp.dot(u.astype(BF16), w_ref[c], preferred_element_type=F32)
            gr = jax.nn.sigmoid(g[:, :LANES] + b_ref[0:1, lanes])
            gi = jax.nn.sigmoid(g[:, LANES:] + b_ref[1:2, lanes])
            neg = -lam_ref[:, lanes]
            softplus = jnp.maximum(neg, 0.0) + jnp.log1p(jnp.exp(-jnp.abs(neg)))
            log_a = gr * ((-RGLRU_C) * softplus)
            a = jnp.exp(log_a)
            x1 = -jnp.tanh(log_a) * (a * a + 1.0)
            m = jnp.where(x1 > 0.0, x1 * lax.rsqrt(x1), 0.0)
            bx = m * (gi * u)
            h = hs[c]
            outs = [None] * SCAN_SUB
            for t in (range(SCAN_SUB - 1, -1, -1) if reverse else range(SCAN_SUB)):
                h = a[t * bsz:(t + 1) * bsz] * h + bx[t * bsz:(t + 1) * bsz]
                outs[t] = h
            hs_ref[c, pl.ds(pl.multiple_of(t0 * bsz, rows_sub), rows_sub), :] = jnp.concatenate(outs, axis=0)
            new_h.append(h)
        return tuple(new_h)

    h_init = tuple(h_ref[:, c * LANES:(c + 1) * LANES] for c in range(n_lb))
    h_fin = lax.fori_loop(0, n_sub, sub_body, h_init)
    for c in range(n_lb):
        lanes = slice(c * LANES, (c + 1) * LANES)
        h_ref[:, lanes] = h_fin[c]
        for b in range(bsz):
            yb = hs_ref[c, pl.ds(b, tc, stride=bsz), :]
            if reverse:
                yb = yb + hf_ref[b, :, lanes]
            y_ref[b, :, lanes] = yb.astype(y_ref.dtype)

    @pl.when(k == nk - 1)
    def _():
        hend_ref[...] = h_ref[...]


def _rglru_dir(p3, col0, conv_w, conv_b, w_gates, b_gates, lam, h0, hf, reverse):
    bsz, seq, _ = p3.shape
    rnn_w = conv_w.shape[-1]
    width = SCAN_WIDTH
    tc = min(SCAN_CHUNK, seq)
    nk = seq // tc
    n_lb = width // LANES
    cblk0 = col0 // width
    halo = 2 * SUBLANES
    hpt = tc // halo
    n_halo_blocks = seq // halo

    def tmap(k):
        return nk - 1 - k if reverse else k

    in_specs = [
        pl.BlockSpec((bsz, halo, width), lambda j, k: (0, jnp.maximum(tmap(k) * hpt - 1, 0), cblk0 + j)),
        pl.BlockSpec((bsz, tc, width), lambda j, k: (0, tmap(k), cblk0 + j)),
        pl.BlockSpec((bsz, halo, width),
                     lambda j, k: (0, jnp.minimum((tmap(k) + 1) * hpt, n_halo_blocks - 1), cblk0 + j)),
        pl.BlockSpec((conv_w.shape[0], width), lambda j, k: (0, j)),
        pl.BlockSpec((1, width), lambda j, k: (0, j)),
        pl.BlockSpec((n_lb, LANES, 2 * LANES), lambda j, k: (j, 0, 0)),
        pl.BlockSpec((2, width), lambda j, k: (0, j)),
        pl.BlockSpec((1, width), lambda j, k: (0, j)),
        pl.BlockSpec((bsz, width), lambda j, k: (0, j)),
    ]
    args = [p3, p3, p3, conv_w, conv_b.reshape(1, rnn_w), w_gates, b_gates, lam.reshape(1, rnn_w), h0]
    if reverse:
        in_specs.append(pl.BlockSpec((bsz, tc, width), lambda j, k: (0, tmap(k), j)))
        args.append(hf)
    out_dtype = BF16 if reverse else F32
    block_bytes = (2 * bsz * tc * width * 2 + 2 * bsz * tc * width * 4 * 2
                   + (tc + 2 * halo) * bsz * width * 4 + tc * bsz * width * 4)
    return pl.pallas_call(
        functools.partial(_rglru_kernel, reverse=reverse, nk=nk),
        out_shape=(jax.ShapeDtypeStruct((bsz, seq, rnn_w), out_dtype),
                   jax.ShapeDtypeStruct((bsz, rnn_w), F32)),
        grid=(rnn_w // width, nk),
        in_specs=in_specs,
        out_specs=(pl.BlockSpec((bsz, tc, width), lambda j, k: (0, tmap(k), j)),
                   pl.BlockSpec((bsz, width), lambda j, k: (0, j))),
        scratch_shapes=[pltpu.VMEM((n_lb, (tc + 2 * halo) * bsz, LANES), F32),
                        pltpu.VMEM((n_lb, tc * bsz, LANES), F32),
                        pltpu.VMEM((bsz, width), F32)],
        compiler_params=_cparams(("parallel", "arbitrary"), block_bytes),
        name="rglru_bwd" if reverse else "rglru_fwd",
    )(*args)


def _rglru(p3, col0, conv_w, conv_b, w_gates, b_gates, lam, h0):
    hf, end_f = _rglru_dir(p3, col0, conv_w, conv_b, w_gates[0], b_gates[0], lam[0], h0[0], None, False)
    y, end_b = _rglru_dir(p3, col0, conv_w, conv_b, w_gates[1], b_gates[1], lam[1], h0[1], hf, True)
    return y, (end_f, end_b)


def _gelu_tanh(x):
    return 0.5 * x * (1.0 + jnp.tanh(0.7978845608028654 * (x + 0.044715 * (x * x * x))))


def _merge_kernel(x_ref, ya_ref, r_ref, gr_ref, ga_ref, gb_ref, gt_ref, woa_ref, wor_ref, wout_ref,
                  o_ref, yr_ref, acc_ref):
    c = pl.program_id(1)

    @pl.when(c == 0)
    def _():
        yr_ref[...] = (r_ref[...].astype(F32) * _gelu_tanh(gr_ref[...].astype(F32))).astype(yr_ref.dtype)
        acc_ref[...] = jnp.zeros_like(acc_ref)

    pa = jnp.dot(ya_ref[...], woa_ref[...], preferred_element_type=F32)
    pr = jnp.dot(yr_ref[...], wor_ref[...], preferred_element_type=F32)
    t = (jax.nn.sigmoid(ga_ref[...].astype(F32)) * pa + jax.nn.sigmoid(gb_ref[...].astype(F32)) * pr)
    acc_ref[...] += jnp.dot(t.astype(BF16), wout_ref[...], preferred_element_type=F32)

    @pl.when(c == pl.num_programs(1) - 1)
    def _():
        o_ref[...] = x_ref[...] + gt_ref[...] * acc_ref[...]


def _merge(xs, ya, r, proj, gr_off, ga_off, gb_off, mods, layer, woa, wor, wout, row_of_tile):
    rows, d = xs.shape
    attn_w = ya.shape[1]
    rnn_w = r.shape[1]
    tm, tc = TM_MERGE, TC_MERGE
    block_bytes = (4 * tm * d * 4 + 2 * tm * (attn_w + 2 * rnn_w + 2 * tc) * 2
                   + 2 * (attn_w + rnn_w + d) * tc * 2 + tm * rnn_w * 2 + tm * d * 4 + 3 * tm * tc * 4)
    return pl.pallas_call(
        _merge_kernel,
        out_shape=jax.ShapeDtypeStruct((rows, d), F32),
        grid=(rows // tm, d // tc),
        in_specs=[
            pl.BlockSpec((tm, d), lambda i, c: (i, 0)),
            pl.BlockSpec((tm, attn_w), lambda i, c: (i, 0)),
            pl.BlockSpec((tm, rnn_w), lambda i, c: (i, 0)),
            pl.BlockSpec((tm, rnn_w), lambda i, c: (i, gr_off // rnn_w)),
            pl.BlockSpec((tm, tc), lambda i, c: (i, ga_off // tc + c)),
            pl.BlockSpec((tm, tc), lambda i, c: (i, gb_off // tc + c)),
            _mod_spec(layer, 5, row_of_tile, d),
            pl.BlockSpec((attn_w, tc), lambda i, c: (0, c)),
            pl.BlockSpec((rnn_w, tc), lambda i, c: (0, c)),
            pl.BlockSpec((tc, d), lambda i, c: (c, 0)),
        ],
        out_specs=pl.BlockSpec((tm, d), lambda i, c: (i, 0)),
        scratch_shapes=[pltpu.VMEM((tm, rnn_w), BF16), pltpu.VMEM((tm, d), F32)],
        input_output_aliases={0: 0},
        compiler_params=_cparams(("parallel", "arbitrary"), block_bytes),
        name="merge",
    )(xs, ya, r, proj, proj, proj, mods, woa, wor, wout)


def _rope_tables(s_len):
    pos = jnp.arange(s_len, dtype=jnp.int32)
    row = (pos // GRID_W).astype(F32)
    col = (pos % GRID_W).astype(F32)
    inv_freq = ROPE_BASE ** (-jnp.arange(ROPE_FREQS, dtype=F32) / ROPE_FREQS)
    ang_r = row[:, None] * inv_freq
    ang_c = col[:, None] * inv_freq
    cos_t = jnp.concatenate([jnp.cos(ang_r), jnp.cos(ang_r), jnp.cos(ang_c), jnp.cos(ang_c)], axis=1)
    sin_t = jnp.concatenate([-jnp.sin(ang_r), jnp.sin(ang_r), -jnp.sin(ang_c), jnp.sin(ang_c)], axis=1)
    return cos_t, sin_t


def _pad_cols(w, n):
    return jnp.pad(w, ((0, 0), (0, n - w.shape[1])))


def kernel(x, c, ctx, c_ctx, w_ada, b_ada, norm_g, final_g, w_ffn_up, w_ffn_down, w_in, attn_sink,
           conv_w, conv_b, rg_w, rg_b, rg_lambda, w_o_attn, w_o_rnn, w_out):
    bsz, s_len, d = x.shape
    c_len = ctx.shape[1]
    depth = w_ada.shape[0]
    d_ff = w_ffn_down.shape[2]
    attn_w = w_o_attn.shape[1]
    rnn_w = w_o_rnn.shape[1]
    kv_w = N_KV_HEADS * HEAD_DIM
    ff_pad = -(-d_ff // TF) * TF
    n_lat, n_ctx = bsz * s_len, bsz * c_len
    assert s_len % TM_FFN == 0 and n_ctx % TM_FFN == 0 and s_len % TM_MERGE == 0 and n_ctx % TM_MERGE == 0
    assert bsz == SUBLANES and bsz + 1 <= MOD_ROWS and rnn_w % SCAN_WIDTH == 0
    assert s_len % SCAN_CHUNK == 0 and c_len % SCAN_SUB == 0 and c_len <= SCAN_CHUNK

    off_k = attn_w
    off_v = off_k + kv_w
    off_xr = off_v + kv_w
    off_gr = off_xr + rnn_w
    off_ga = off_gr + rnn_w
    off_gb = off_ga + d
    p_xr, p_gr, p_ga, p_gb = attn_w, attn_w + rnn_w, attn_w + 2 * rnn_w, attn_w + 2 * rnn_w + d
    p_k = p_gb + d
    assert p_gr % rnn_w == 0 and p_ga % TC_MERGE == 0 and p_gb % TC_MERGE == 0
    assert p_k % kv_w == 0 and p_xr % SCAN_WIDTH == 0 and (p_k + 2 * kv_w) % TN_PROJ == 0

    def lat_row(tm):
        return lambda i: i // (s_len // tm)

    def ctx_row(i):
        return bsz

    c_all = jnp.zeros((MOD_ROWS, d), F32).at[:bsz].set(c).at[bsz].set(c_ctx)
    mods = _adaln(c_all, w_ada, b_ada).reshape(depth, MOD_ROWS, 1, N_MOD * d)
    cos_t, sin_t = _rope_tables(s_len)
    xl = x.reshape(n_lat, d)
    xc = ctx.reshape(n_ctx, d)
    h_zero = (jnp.zeros((bsz, rnn_w), F32), jnp.zeros((bsz, rnn_w), F32))

    for l in range(depth):
        need_ctx = l < depth - 1

        def ffn_weights(s):
            up = w_ffn_up[l, s]
            wg = _pad_cols(up[:, :d_ff], ff_pad).astype(BF16)
            wu = _pad_cols(up[:, d_ff:], ff_pad).astype(BF16)
            wd = jnp.pad(w_ffn_down[l, s], ((0, ff_pad - d_ff), (0, 0))).astype(BF16)
            return wg, wu, wd

        w1 = ffn_weights(0)
        xl = _ffn(xl, mods, l, 0, norm_g[l, 0], *w1, lat_row(TM_FFN))
        xc = _ffn(xc, mods, l, 0, norm_g[l, 0], *w1, ctx_row)

        wi = w_in[l]
        w_proj = jnp.concatenate([wi[:, :off_k], wi[:, off_xr:off_gr], wi[:, off_gr:off_ga],
                                  wi[:, off_ga:off_gb], wi[:, off_gb:], wi[:, off_k:off_v],
                                  wi[:, off_v:off_xr]], axis=1).astype(BF16)
        p_lat = _proj(xl, mods, l, norm_g[l, 1], w_proj, lat_row(TM_FFN))
        p_ctx = _proj(xc, mods, l, norm_g[l, 1], w_proj, ctx_row)

        sink_b = jnp.broadcast_to(attn_sink[l][:, None], (attn_sink.shape[1], LANES))
        ya_lat, ya_ctx = _attention(p_lat, p_ctx, cos_t, sin_t, sink_b, bsz, s_len, c_len, attn_w, p_k,
                                    need_ctx)

        w_gates = jnp.concatenate([rg_w[l, :, 0], rg_w[l, :, 1]], axis=-1).astype(BF16)
        r_ctx, h_end = _rglru(p_ctx.reshape(bsz, c_len, -1), p_xr, conv_w[l], conv_b[l], w_gates,
                              rg_b[l], rg_lambda[l], h_zero)
        r_lat, _ = _rglru(p_lat.reshape(bsz, s_len, -1), p_xr, conv_w[l], conv_b[l], w_gates,
                          rg_b[l], rg_lambda[l], h_end)

        woa, wor, wout = w_o_attn[l].astype(BF16), w_o_rnn[l].astype(BF16), w_out[l].astype(BF16)
        xl = _merge(xl, ya_lat, r_lat.reshape(n_lat, rnn_w), p_lat, p_gr, p_ga, p_gb, mods, l,
                    woa, wor, wout, lat_row(TM_MERGE))
        w2 = ffn_weights(1)
        if need_ctx:
            xc = _merge(xc, ya_ctx, r_ctx.reshape(n_ctx, rnn_w), p_ctx, p_gr, p_ga, p_gb, mods, l,
                        woa, wor, wout, ctx_row)
            xc = _ffn(xc, mods, l, 6, norm_g[l, 2], *w2, ctx_row)
        xl = _ffn(xl, mods, l, 6, norm_g[l, 2], *w2, lat_row(TM_FFN),
                  final_g=None if need_ctx else final_g)

    return xl.reshape(bsz, s_len, d)
```

```python
import functools

import jax
import jax.numpy as jnp
from jax import lax
from jax.experimental import pallas as pl
from jax.experimental.pallas import tpu as pltpu

F32 = jnp.float32
BF16 = jnp.bfloat16

LANES = 128
SUBLANES = 8
VMEM_PHYSICAL_BYTES = 64 * 1024 * 1024
VMEM_HEADROOM_BYTES = 4 * 1024 * 1024

HEAD_DIM = 128
N_KV_HEADS = 2
WINDOW_BLOCK = 128
GRID_W = 64
ROPE_BASE = 10000.0
ROPE_FREQS = HEAD_DIM // 4
CONV_LEFT = 2
RGLRU_C = 8.0
N_MOD = 9
EPS = 1e-6
MOD_ROWS = 16
NEG_BIG = -1e30

TM_FFN = 1024
TF = 512
TN_PROJ = 1536
TM_MERGE = 512
TC_MERGE = 1024
TN_ADA = 1024
NORM_ROWS = 128
SCAN_CHUNK = 256
SCAN_SUB = 32
SCAN_WIDTH = 512
ATTN_Q_BLOCKS = 2


def _cparams(sem, block_bytes):
    limit = min(int(block_bytes) + VMEM_HEADROOM_BYTES, VMEM_PHYSICAL_BYTES - VMEM_HEADROOM_BYTES // 2)
    return pltpu.CompilerParams(dimension_semantics=sem, vmem_limit_bytes=limit)


def _adaln_kernel(c_ref, w_ref, b_ref, o_ref):
    c = c_ref[...]
    sc = c * jax.nn.sigmoid(c)
    o_ref[...] = jnp.dot(sc, w_ref[...], preferred_element_type=F32,
                         precision=lax.Precision.HIGHEST) + b_ref[...]


def _adaln(c_all, w_ada, b_ada):
    depth, d, n = w_ada.shape
    return pl.pallas_call(
        _adaln_kernel,
        out_shape=jax.ShapeDtypeStruct((depth, MOD_ROWS, n), F32),
        grid=(depth, n // TN_ADA),
        in_specs=[
            pl.BlockSpec((MOD_ROWS, d), lambda l, j: (0, 0)),
            pl.BlockSpec((None, d, TN_ADA), lambda l, j: (l, 0, j)),
            pl.BlockSpec((None, 1, TN_ADA), lambda l, j: (l, 0, j)),
        ],
        out_specs=pl.BlockSpec((None, MOD_ROWS, TN_ADA), lambda l, j: (l, 0, j)),
        compiler_params=_cparams(("parallel", "parallel"), 2 * d * TN_ADA * 4 + 4 * MOD_ROWS * (d + TN_ADA) * 4),
        name="adaln",
    )(c_all, w_ada, b_ada.reshape(depth, 1, n))


def _rms_scale(x):
    return x * lax.rsqrt(jnp.mean(x * x, axis=-1, keepdims=True) + EPS)


def _norm_mod_store(xn_ref, x_ref, g_ref, sh_ref, sc_ref, zero_ref=None):
    g = g_ref[...]
    sh = sh_ref[...]
    sc1 = 1.0 + sc_ref[...]

    def body(i, carry):
        r0 = pl.multiple_of(i * NORM_ROWS, NORM_ROWS)
        y = _rms_scale(x_ref[pl.ds(r0, NORM_ROWS), :]) * g
        xn_ref[pl.ds(r0, NORM_ROWS), :] = (y * sc1 + sh).astype(xn_ref.dtype)
        if zero_ref is not None:
            zero_ref[pl.ds(r0, NORM_ROWS), :] = jnp.zeros((NORM_ROWS, zero_ref.shape[1]), zero_ref.dtype)
        return carry

    lax.fori_loop(0, x_ref.shape[0] // NORM_ROWS, body, 0)


def _mod_spec(layer, k, row_of_tile, d):
    return pl.BlockSpec((None, None, 1, d), lambda i, j: (layer, row_of_tile(i), 0, k))


def _ffn_kernel(x_ref, g_ref, sh_ref, sc_ref, gt_ref, wg_ref, wu_ref, wd_ref, *refs, final_norm):
    if final_norm:
        fg_ref, o_ref, xn_ref = refs
    else:
        o_ref, xn_ref = refs
    c = pl.program_id(1)

    @pl.when(c == 0)
    def _():
        _norm_mod_store(xn_ref, x_ref, g_ref, sh_ref, sc_ref, zero_ref=o_ref)

    xn = xn_ref[...]
    h = jnp.dot(xn, wg_ref[...], preferred_element_type=F32)
    u = jnp.dot(xn, wu_ref[...], preferred_element_type=F32)
    a = ((h * jax.nn.sigmoid(h)) * u).astype(BF16)
    o_ref[...] += jnp.dot(a, wd_ref[...], preferred_element_type=F32)

    @pl.when(c == pl.num_programs(1) - 1)
    def _():
        gate = 0.5 * gt_ref[...]

        def body(i, carry):
            r0 = pl.multiple_of(i * NORM_ROWS, NORM_ROWS)
            y = x_ref[pl.ds(r0, NORM_ROWS), :] + gate * o_ref[pl.ds(r0, NORM_ROWS), :]
            if final_norm:
                y = _rms_scale(y) * fg_ref[...]
            o_ref[pl.ds(r0, NORM_ROWS), :] = y
            return carry

        lax.fori_loop(0, x_ref.shape[0] // NORM_ROWS, body, 0)


def _ffn(xs, mods, layer, half, g, wg, wu, wd, row_of_tile, final_g=None, in_place=True):
    rows, d = xs.shape
    ff = wg.shape[-1]
    tm = TM_FFN
    mod_k0 = 6 * half
    final_norm = final_g is not None
    in_specs = [
        pl.BlockSpec((tm, d), lambda i, c: (i, 0)),
        pl.BlockSpec((1, d), lambda i, c: (0, 0)),
        _mod_spec(layer, mod_k0, row_of_tile, d),
        _mod_spec(layer, mod_k0 + 1, row_of_tile, d),
        _mod_spec(layer, mod_k0 + 2, row_of_tile, d),
        pl.BlockSpec((None, None, d, TF), lambda i, c: (layer, half, 0, c)),
        pl.BlockSpec((None, None, d, TF), lambda i, c: (layer, half, 0, c)),
        pl.BlockSpec((None, None, TF, d), lambda i, c: (layer, half, c, 0)),
    ]
    args = [xs, g.reshape(1, d), mods, mods, mods, wg, wu, wd]
    if final_norm:
        in_specs.append(pl.BlockSpec((1, d), lambda i, c: (0, 0)))
        args.append(final_g.reshape(1, d))
    block_bytes = (2 * tm * d * 4 + 2 * tm * d * 4 + 2 * 3 * d * TF * 2 + tm * d * 2
                   + 2 * tm * TF * 4 + tm * TF * 2)
    return pl.pallas_call(
        functools.partial(_ffn_kernel, final_norm=final_norm),
        out_shape=jax.ShapeDtypeStruct((rows, d), F32),
        grid=(rows // tm, ff // TF),
        in_specs=in_specs,
        out_specs=pl.BlockSpec((tm, d), lambda i, c: (i, 0)),
        scratch_shapes=[pltpu.VMEM((tm, d), BF16)],
        input_output_aliases={0: 0} if (in_place and not final_norm) else {},
        compiler_params=_cparams(("parallel", "arbitrary"), block_bytes),
        name="ffn_final" if final_norm else "ffn",
    )(*args)


def _rope(t, cs, sn):
    lane = lax.broadcasted_iota(jnp.int32, t.shape, 1)
    first_half = (lane & (2 * ROPE_FREQS - 1)) < ROPE_FREQS
    partner = jnp.where(first_half,
                        pltpu.roll(t, HEAD_DIM - ROPE_FREQS, axis=1),
                        pltpu.roll(t, ROPE_FREQS, axis=1))
    return t * cs + partner * sn


def _proj_kernel(x_ref, g_ref, sh_ref, sc_ref, w_ref, *refs, rope_cols):
    if rope_cols:
        cos_ref, sin_ref, o_ref, xn_ref = refs
    else:
        o_ref, xn_ref = refs
    j = pl.program_id(1)

    @pl.when(j == 0)
    def _():
        _norm_mod_store(xn_ref, x_ref, g_ref, sh_ref, sc_ref)

    def plain():
        o_ref[...] = jnp.dot(xn_ref[...], w_ref[...], preferred_element_type=F32).astype(o_ref.dtype)

    if not rope_cols:
        plain()
        return

    @pl.when(j == 0)
    def _():
        xn = xn_ref[...]
        cs = cos_ref[...]
        sn = sin_ref[...]
        pair = 2 * HEAD_DIM
        for c0 in range(0, o_ref.shape[1], pair):
            res = jnp.dot(xn, w_ref[:, c0:c0 + pair], preferred_element_type=F32)
            for h0 in range(0, pair, HEAD_DIM):
                t = res[:, h0:h0 + HEAD_DIM]
                if c0 + h0 < rope_cols:
                    t = _rope(t, cs, sn)
                o_ref[:, c0 + h0:c0 + h0 + HEAD_DIM] = t.astype(o_ref.dtype)

    pl.when(j != 0)(plain)


def _proj(xs, mods, layer, g, w, row_of_tile, rope=None):
    rows, d = xs.shape
    n = w.shape[-1]
    tm = TM_FFN
    in_specs = [
        pl.BlockSpec((tm, d), lambda i, j: (i, 0)),
        pl.BlockSpec((1, d), lambda i, j: (0, 0)),
        _mod_spec(layer, 3, row_of_tile, d),
        _mod_spec(layer, 4, row_of_tile, d),
        pl.BlockSpec((None, d, TN_PROJ), lambda i, j: (layer, 0, j)),
    ]
    args = [xs, g.reshape(1, d), mods, mods, w]
    rope_cols = 0
    if rope is not None:
        cos_t, sin_t, rope_cols, tiles_per_sample = rope
        assert rope_cols <= TN_PROJ
        in_specs += [pl.BlockSpec((tm, HEAD_DIM), lambda i, j: (i % tiles_per_sample, 0))] * 2
        args += [cos_t, sin_t]
    block_bytes = (2 * tm * d * 4 + tm * d * 2 + 2 * d * TN_PROJ * 2 + 2 * tm * TN_PROJ * 2
                   + tm * TN_PROJ * 4 + 4 * tm * HEAD_DIM * 4)
    return pl.pallas_call(
        functools.partial(_proj_kernel, rope_cols=rope_cols),
        out_shape=jax.ShapeDtypeStruct((rows, n), BF16),
        grid=(rows // tm, n // TN_PROJ),
        in_specs=in_specs,
        out_specs=pl.BlockSpec((tm, TN_PROJ), lambda i, j: (i, j)),
        scratch_shapes=[pltpu.VMEM((tm, d), BF16)],
        compiler_params=_cparams(("parallel", "arbitrary"), block_bytes),
        name="in_proj_rope" if rope_cols else "in_proj",
    )(*args)


def _softmax_pv(q4, kcat, vcat, sk, bias, scale):
    s = lax.dot_general(q4, kcat, (((1,), (1,)), ((), ())), preferred_element_type=F32) * scale
    if bias is not None:
        rb = bias.shape[0]
        s = jnp.concatenate([s[r0:r0 + rb] + bias for r0 in range(0, s.shape[0], rb)], axis=0)
    m = jnp.maximum(jnp.max(s, axis=-1, keepdims=True), sk)
    p = jnp.exp(s - m)
    denom = jnp.sum(p, axis=-1, keepdims=True) + jnp.exp(sk - m)
    o = jnp.dot(p.astype(BF16), vcat, preferred_element_type=F32)
    return o / denom


def _sink_rows(sink_ref, g, q_per_kv, rows):
    parts = [jnp.broadcast_to(sink_ref[g * q_per_kv + j:g * q_per_kv + j + 1, 0:1], (rows, 1))
             for j in range(q_per_kv)]
    return jnp.concatenate(parts, axis=0)


def _attn_lat_kernel(q_ref, kp_ref, kc_ref, kn_ref, vp_ref, vc_ref, vn_ref, kx_ref, vx_ref,
                     sink_ref, o_ref, *, nb, q_per_kv, scale):
    n = pl.program_id(1)
    blk = WINDOW_BLOCK
    dh = HEAD_DIM
    c_len = kx_ref.shape[0]
    nq = q_ref.shape[0] // blk
    keys = 3 * blk + c_len
    ri = lax.broadcasted_iota(jnp.int32, (blk, keys), 0)
    kj = lax.broadcasted_iota(jnp.int32, (blk, keys), 1)
    band = (kj >= 3 * blk) | ((kj >= ri) & (kj <= ri + 2 * blk))

    for qb in range(nq):
        rows = slice(qb * blk, (qb + 1) * blk)
        first = (n == 0) if qb == 0 else False
        last = (n == nb // nq - 1) if qb == nq - 1 else False
        lo = jnp.where(first, blk, 0)
        hi = jnp.where(last, 2 * blk, 3 * blk)
        bias = jnp.where(band & ((kj >= 3 * blk) | ((kj >= lo) & (kj < hi))), 0.0, NEG_BIG).astype(F32)

        def window(p_ref, c_ref, n_ref, x_ref, cols):
            parts = []
            for w in (qb - 1, qb, qb + 1):
                if w < 0:
                    parts.append(p_ref[:, cols])
                elif w >= nq:
                    parts.append(n_ref[:, cols])
                else:
                    parts.append(c_ref[w * blk:(w + 1) * blk, cols])
            return jnp.concatenate(parts + [x_ref[:, cols]], axis=0)

        for g in range(N_KV_HEADS):
            cols = slice(g * dh, (g + 1) * dh)
            kcat = window(kp_ref, kc_ref, kn_ref, kx_ref, cols)
            vcat = window(vp_ref, vc_ref, vn_ref, vx_ref, cols)
            q4 = jnp.concatenate([q_ref[rows, (g * q_per_kv + j) * dh:(g * q_per_kv + j + 1) * dh]
                                  for j in range(q_per_kv)], axis=0)
            sk = _sink_rows(sink_ref, g, q_per_kv, blk)
            o = _softmax_pv(q4, kcat, vcat, sk, bias, scale)
            for j in range(q_per_kv):
                h = g * q_per_kv + j
                o_ref[rows, h * dh:(h + 1) * dh] = o[j * blk:(j + 1) * blk].astype(o_ref.dtype)


def _attn_ctx_kernel(q_ref, kx_ref, vx_ref, sink_ref, o_ref, *, q_per_kv, scale):
    dh = HEAD_DIM
    c_len = q_ref.shape[0]
    for g in range(N_KV_HEADS):
        cols = slice(g * dh, (g + 1) * dh)
        q4 = jnp.concatenate([q_ref[:, (g * q_per_kv + j) * dh:(g * q_per_kv + j + 1) * dh]
                              for j in range(q_per_kv)], axis=0)
        sk = _sink_rows(sink_ref, g, q_per_kv, c_len)
        o = _softmax_pv(q4, kx_ref[:, cols], vx_ref[:, cols], sk, None, scale)
        for j in range(q_per_kv):
            h = g * q_per_kv + j
            o_ref[:, h * dh:(h + 1) * dh] = o[j * c_len:(j + 1) * c_len].astype(o_ref.dtype)


def _attention(p_lat, p_ctx, sink_b, bsz, s_len, c_len, attn_w, k_off, need_ctx):
    kv_w = N_KV_HEADS * HEAD_DIM
    n_q_heads = attn_w // HEAD_DIM
    q_per_kv = n_q_heads // N_KV_HEADS
    blk = WINDOW_BLOCK
    nb = s_len // blk
    k_col = k_off // kv_w
    v_col = k_col + 1
    scale = HEAD_DIM ** -0.5

    nq = ATTN_Q_BLOCKS
    steps = nb // nq
    assert nb % nq == 0

    def kv_specs(col):
        prev = pl.BlockSpec((blk, kv_w), lambda b, n: (b * nb + jnp.maximum(n * nq - 1, 0), col))
        own = pl.BlockSpec((nq * blk, kv_w), lambda b, n: (b * steps + n, col))
        nxt = pl.BlockSpec((blk, kv_w), lambda b, n: (b * nb + jnp.minimum((n + 1) * nq, nb - 1), col))
        return [prev, own, nxt]

    ya_lat = pl.pallas_call(
        functools.partial(_attn_lat_kernel, nb=nb, q_per_kv=q_per_kv, scale=scale),
        out_shape=jax.ShapeDtypeStruct((bsz * s_len, attn_w), BF16),
        grid=(bsz, steps),
        in_specs=[pl.BlockSpec((nq * blk, attn_w), lambda b, n: (b * steps + n, 0))]
        + kv_specs(k_col) + kv_specs(v_col) + [
            pl.BlockSpec((c_len, kv_w), lambda b, n: (b, k_col)),
            pl.BlockSpec((c_len, kv_w), lambda b, n: (b, v_col)),
            pl.BlockSpec((n_q_heads, LANES), lambda b, n: (0, 0)),
        ],
        out_specs=pl.BlockSpec((nq * blk, attn_w), lambda b, n: (b * steps + n, 0)),
        compiler_params=_cparams(("parallel", "parallel"), 16 * 1024 * 1024),
        name="attn_latent",
    )(p_lat, p_lat, p_lat, p_lat, p_lat, p_lat, p_lat, p_ctx, p_ctx, sink_b)
    if not need_ctx:
        return ya_lat, None
    ya_ctx = pl.pallas_call(
        functools.partial(_attn_ctx_kernel, q_per_kv=q_per_kv, scale=scale),
        out_shape=jax.ShapeDtypeStruct((bsz * c_len, attn_w), BF16),
        grid=(bsz,),
        in_specs=[
            pl.BlockSpec((c_len, attn_w), lambda b: (b, 0)),
            pl.BlockSpec((c_len, kv_w), lambda b: (b, k_col)),
            pl.BlockSpec((c_len, kv_w), lambda b: (b, v_col)),
            pl.BlockSpec((n_q_heads, LANES), lambda b: (0, 0)),
        ],
        out_specs=pl.BlockSpec((c_len, attn_w), lambda b: (b, 0)),
        compiler_params=_cparams(("parallel",), 16 * 1024 * 1024),
        name="attn_context",
    )(p_ctx, p_ctx, p_ctx, sink_b)
    return ya_lat, ya_ctx


def _rglru_kernel(xp_ref, x_ref, xn_ref, cw_ref, cb_ref, w_ref, b_ref, lam_ref, h0_ref, *refs,
                  reverse, nk):
    if reverse:
        hf_ref, y_ref, hend_ref, xt_ref, hs_ref, h_ref = refs
    else:
        y_ref, hend_ref, xt_ref, hs_ref, h_ref = refs
    k = pl.program_id(1)
    ci = nk - 1 - k if reverse else k
    bsz, tc, width = x_ref.shape
    n_lb = width // LANES
    halo = xp_ref.shape[1]
    n_sub = tc // SCAN_SUB
    rows_sub = SCAN_SUB * bsz

    @pl.when(k == 0)
    def _():
        h_ref[...] = h0_ref[...]

    for c in range(n_lb):
        lanes = slice(c * LANES, (c + 1) * LANES)
        for b in range(bsz):
            xt_ref[c, pl.ds(b, halo, stride=bsz), :] = xp_ref[b, :, lanes].astype(F32)
            xt_ref[c, pl.ds(halo * bsz + b, tc, stride=bsz), :] = x_ref[b, :, lanes].astype(F32)
            xt_ref[c, pl.ds((halo + tc) * bsz + b, halo, stride=bsz), :] = xn_ref[b, :, lanes].astype(F32)

    @pl.when(ci == 0)
    def _():
        xt_ref[:, 0:halo * bsz, :] = jnp.zeros((n_lb, halo * bsz, LANES), F32)

    @pl.when(ci == nk - 1)
    def _():
        xt_ref[:, (halo + tc) * bsz:(2 * halo + tc) * bsz, :] = jnp.zeros((n_lb, halo * bsz, LANES), F32)

    def sub_body(i, hs):
        si = (n_sub - 1 - i) if reverse else i
        t0 = si * SCAN_SUB
        new_h = []
        for c in range(n_lb):
            lanes = slice(c * LANES, (c + 1) * LANES)
            cw = cw_ref[:, lanes]
            u = cb_ref[:, lanes]
            for kk in range(cw.shape[0]):
                r0 = pl.multiple_of((t0 + halo - CONV_LEFT + kk) * bsz, bsz)
                u = u + cw[kk:kk + 1, :] * xt_ref[c, pl.ds(r0, rows_sub), :]
            g = jnp.dot(u.astype(BF16), w_ref[c], preferred_element_type=F32)
            gr = jax.nn.sigmoid(g[:, :LANES] + b_ref[0:1, lanes])
            gi = jax.nn.sigmoid(g[:, LANES:] + b_ref[1:2, lanes])
            neg = -lam_ref[:, lanes]
            softplus = jnp.maximum(neg, 0.0) + jnp.log1p(jnp.exp(-jnp.abs(neg)))
            log_a = gr * ((-RGLRU_C) * softplus)
            a = jnp.exp(log_a)
            x1 = -jnp.tanh(log_a) * (a * a + 1.0)
            m = jnp.where(x1 > 0.0, x1 * lax.rsqrt(x1), 0.0)
            bx = m * (gi * u)
            h = hs[c]
            outs = [None] * SCAN_SUB
            for t in (range(SCAN_SUB - 1, -1, -1) if reverse else range(SCAN_SUB)):
                h = a[t * bsz:(t + 1) * bsz] * h + bx[t * bsz:(t + 1) * bsz]
                outs[t] = h
            hs_ref[c, pl.ds(pl.multiple_of(t0 * bsz, rows_sub), rows_sub), :] = jnp.concatenate(outs, axis=0)
            new_h.append(h)
        return tuple(new_h)

    h_init = tuple(h_ref[:, c * LANES:(c + 1) * LANES] for c in range(n_lb))
    h_fin = lax.fori_loop(0, n_sub, sub_body, h_init)
    for c in range(n_lb):
        lanes = slice(c * LANES, (c + 1) * LANES)
        h_ref[:, lanes] = h_fin[c]
        for b in range(bsz):
            yb = hs_ref[c, pl.ds(b, tc, stride=bsz), :]
            if reverse:
                yb = yb + hf_ref[b, :, lanes]
            y_ref[b, :, lanes] = yb.astype(y_ref.dtype)

    @pl.when(k == nk - 1)
    def _():
        hend_ref[...] = h_ref[...]


def _rglru_dir(p3, col0, layer, conv_w, conv_b, w_gates, b_gates, lam, h0, hf, reverse):
    direction = 1 if reverse else 0
    bsz, seq, _ = p3.shape
    rnn_w = conv_w.shape[-1]
    width = SCAN_WIDTH
    tc = min(SCAN_CHUNK, seq)
    nk = seq // tc
    n_lb = width // LANES
    cblk0 = col0 // width
    halo = 2 * SUBLANES
    hpt = tc // halo
    n_halo_blocks = seq // halo

    def tmap(k):
        return nk - 1 - k if reverse else k

    in_specs = [
        pl.BlockSpec((bsz, halo, width), lambda j, k: (0, jnp.maximum(tmap(k) * hpt - 1, 0), cblk0 + j)),
        pl.BlockSpec((bsz, tc, width), lambda j, k: (0, tmap(k), cblk0 + j)),
        pl.BlockSpec((bsz, halo, width),
                     lambda j, k: (0, jnp.minimum((tmap(k) + 1) * hpt, n_halo_blocks - 1), cblk0 + j)),
        pl.BlockSpec((conv_w.shape[0], width), lambda j, k: (0, j)),
        pl.BlockSpec((1, width), lambda j, k: (0, j)),
        pl.BlockSpec((None, None, n_lb, LANES, 2 * LANES), lambda j, k: (layer, direction, j, 0, 0)),
        pl.BlockSpec((2, width), lambda j, k: (0, j)),
        pl.BlockSpec((1, width), lambda j, k: (0, j)),
        pl.BlockSpec((bsz, width), lambda j, k: (0, j)),
    ]
    args = [p3, p3, p3, conv_w, conv_b.reshape(1, rnn_w), w_gates, b_gates, lam.reshape(1, rnn_w), h0]
    if reverse:
        in_specs.append(pl.BlockSpec((bsz, tc, width), lambda j, k: (0, tmap(k), j)))
        args.append(hf)
    out_dtype = BF16 if reverse else F32
    block_bytes = (2 * bsz * tc * width * 2 + 2 * bsz * tc * width * 4 * 2
                   + (tc + 2 * halo) * bsz * width * 4 + tc * bsz * width * 4)
    return pl.pallas_call(
        functools.partial(_rglru_kernel, reverse=reverse, nk=nk),
        out_shape=(jax.ShapeDtypeStruct((bsz, seq, rnn_w), out_dtype),
                   jax.ShapeDtypeStruct((bsz, rnn_w), F32)),
        grid=(rnn_w // width, nk),
        in_specs=in_specs,
        out_specs=(pl.BlockSpec((bsz, tc, width), lambda j, k: (0, tmap(k), j)),
                   pl.BlockSpec((bsz, width), lambda j, k: (0, j))),
        scratch_shapes=[pltpu.VMEM((n_lb, (tc + 2 * halo) * bsz, LANES), F32),
                        pltpu.VMEM((n_lb, tc * bsz, LANES), F32),
                        pltpu.VMEM((bsz, width), F32)],
        compiler_params=_cparams(("parallel", "arbitrary"), block_bytes),
        name="rglru_bwd" if reverse else "rglru_fwd",
    )(*args)


def _rglru(p3, col0, layer, conv_w, conv_b, w_gates, b_gates, lam, h0):
    hf, end_f = _rglru_dir(p3, col0, layer, conv_w, conv_b, w_gates, b_gates[0], lam[0], h0[0], None, False)
    y, end_b = _rglru_dir(p3, col0, layer, conv_w, conv_b, w_gates, b_gates[1], lam[1], h0[1], hf, True)
    return y, (end_f, end_b)


def _gelu_tanh(x):
    return 0.5 * x * (1.0 + jnp.tanh(0.7978845608028654 * (x + 0.044715 * (x * x * x))))


def _merge_kernel(x_ref, ya_ref, r_ref, gr0_ref, gr1_ref, ga0_ref, ga1_ref, gb0_ref, gb1_ref, gt_ref,
                  woa_ref, wor_ref, wout_ref, o_ref, yr_ref, acc_ref):
    c = pl.program_id(1)

    def wide(lo_ref, hi_ref):
        return jnp.concatenate([lo_ref[...], hi_ref[...]], axis=1).astype(F32)

    @pl.when(c == 0)
    def _():
        yr_ref[...] = (r_ref[...].astype(F32) * _gelu_tanh(wide(gr0_ref, gr1_ref))).astype(yr_ref.dtype)
        acc_ref[...] = jnp.zeros_like(acc_ref)

    pa = jnp.dot(ya_ref[...], woa_ref[...], preferred_element_type=F32)
    pr = jnp.dot(yr_ref[...], wor_ref[...], preferred_element_type=F32)
    t = jax.nn.sigmoid(wide(ga0_ref, ga1_ref)) * pa + jax.nn.sigmoid(wide(gb0_ref, gb1_ref)) * pr
    acc_ref[...] += jnp.dot(t.astype(BF16), wout_ref[...], preferred_element_type=F32)

    @pl.when(c == pl.num_programs(1) - 1)
    def _():
        o_ref[...] = x_ref[...] + gt_ref[...] * acc_ref[...]


def _merge(xs, ya, r, proj, gr_off, ga_off, gb_off, mods, layer, woa, wor, wout, row_of_tile):
    rows, d = xs.shape
    attn_w = ya.shape[1]
    rnn_w = r.shape[1]
    tm, tc = TM_MERGE, TC_MERGE
    half = tc // 2
    assert rnn_w == tc and gr_off % half == 0 and ga_off % half == 0 and gb_off % half == 0

    def half_spec(off, k):
        return pl.BlockSpec((tm, half), lambda i, c: (i, off // half + 2 * c + k))

    def gr_spec(k):
        return pl.BlockSpec((tm, half), lambda i, c: (i, gr_off // half + k))

    block_bytes = (4 * tm * d * 4 + 2 * tm * (attn_w + 2 * rnn_w + 2 * tc) * 2
                   + 2 * (attn_w + rnn_w + d) * tc * 2 + tm * rnn_w * 2 + tm * d * 4 + 3 * tm * tc * 4)
    return pl.pallas_call(
        _merge_kernel,
        out_shape=jax.ShapeDtypeStruct((rows, d), F32),
        grid=(rows // tm, d // tc),
        in_specs=[
            pl.BlockSpec((tm, d), lambda i, c: (i, 0)),
            pl.BlockSpec((tm, attn_w), lambda i, c: (i, 0)),
            pl.BlockSpec((tm, rnn_w), lambda i, c: (i, 0)),
            gr_spec(0), gr_spec(1),
            half_spec(ga_off, 0), half_spec(ga_off, 1),
            half_spec(gb_off, 0), half_spec(gb_off, 1),
            _mod_spec(layer, 5, row_of_tile, d),
            pl.BlockSpec((None, attn_w, tc), lambda i, c: (layer, 0, c)),
            pl.BlockSpec((None, rnn_w, tc), lambda i, c: (layer, 0, c)),
            pl.BlockSpec((None, tc, d), lambda i, c: (layer, c, 0)),
        ],
        out_specs=pl.BlockSpec((tm, d), lambda i, c: (i, 0)),
        scratch_shapes=[pltpu.VMEM((tm, rnn_w), BF16), pltpu.VMEM((tm, d), F32)],
        input_output_aliases={0: 0},
        compiler_params=_cparams(("parallel", "arbitrary"), block_bytes),
        name="merge",
    )(xs, ya, r, proj, proj, proj, proj, proj, proj, mods, woa, wor, wout)


def _rope_tables(s_len):
    pos = jnp.arange(s_len, dtype=jnp.int32)
    row = (pos // GRID_W).astype(F32)
    col = (pos % GRID_W).astype(F32)
    inv_freq = ROPE_BASE ** (-jnp.arange(ROPE_FREQS, dtype=F32) / ROPE_FREQS)
    ang_r = row[:, None] * inv_freq
    ang_c = col[:, None] * inv_freq
    cos_t = jnp.concatenate([jnp.cos(ang_r), jnp.cos(ang_r), jnp.cos(ang_c), jnp.cos(ang_c)], axis=1)
    sin_t = jnp.concatenate([-jnp.sin(ang_r), jnp.sin(ang_r), -jnp.sin(ang_c), jnp.sin(ang_c)], axis=1)
    return cos_t, sin_t


def kernel(x, c, ctx, c_ctx, w_ada, b_ada, norm_g, final_g, w_ffn_up, w_ffn_down, w_in, attn_sink,
           conv_w, conv_b, rg_w, rg_b, rg_lambda, w_o_attn, w_o_rnn, w_out):
    bsz, s_len, d = x.shape
    c_len = ctx.shape[1]
    depth = w_ada.shape[0]
    d_ff = w_ffn_down.shape[2]
    attn_w = w_o_attn.shape[1]
    rnn_w = w_o_rnn.shape[1]
    kv_w = N_KV_HEADS * HEAD_DIM
    ff_pad = -(-d_ff // TF) * TF
    n_lat, n_ctx = bsz * s_len, bsz * c_len
    assert s_len % TM_FFN == 0 and n_ctx % TM_FFN == 0 and s_len % TM_MERGE == 0 and n_ctx % TM_MERGE == 0
    assert bsz == SUBLANES and bsz + 1 <= MOD_ROWS and rnn_w % SCAN_WIDTH == 0
    assert s_len % SCAN_CHUNK == 0 and c_len % SCAN_SUB == 0 and c_len <= SCAN_CHUNK

    off_k = attn_w
    off_xr = off_k + 2 * kv_w
    off_gr = off_xr + rnn_w
    off_ga = off_gr + rnn_w
    off_gb = off_ga + d
    assert off_xr == TN_PROJ and off_k % kv_w == 0 and off_xr % SCAN_WIDTH == 0

    def lat_row(tm):
        return lambda i: i // (s_len // tm)

    def ctx_row(i):
        return bsz

    pad_ff = ((0, 0), (0, 0), (0, 0), (0, ff_pad - d_ff))
    wg_all = jnp.pad(w_ffn_up[..., :d_ff], pad_ff).astype(BF16)
    wu_all = jnp.pad(w_ffn_up[..., d_ff:], pad_ff).astype(BF16)
    wd_all = jnp.pad(w_ffn_down, ((0, 0), (0, 0), (0, ff_pad - d_ff), (0, 0))).astype(BF16)
    w_in_all = w_in.astype(BF16)
    woa_all, wor_all, wout_all = w_o_attn.astype(BF16), w_o_rnn.astype(BF16), w_out.astype(BF16)
    w_gates_all = jnp.concatenate([rg_w[:, :, 0], rg_w[:, :, 1]], axis=-1).astype(BF16)

    c_all = jnp.zeros((MOD_ROWS, d), F32).at[:bsz].set(c).at[bsz].set(c_ctx)
    mods = _adaln(c_all, w_ada, b_ada).reshape(depth, MOD_ROWS, 1, N_MOD * d)
    cos_t, sin_t = _rope_tables(s_len)
    rope = (cos_t, sin_t, attn_w + kv_w, s_len // TM_FFN)
    xl = x.reshape(n_lat, d)
    xc = ctx.reshape(n_ctx, d)
    h_zero = (jnp.zeros((bsz, rnn_w), F32), jnp.zeros((bsz, rnn_w), F32))

    for l in range(depth):
        need_ctx = l < depth - 1
        first = l == 0
        xl = _ffn(xl, mods, l, 0, norm_g[l, 0], wg_all, wu_all, wd_all, lat_row(TM_FFN), in_place=not first)
        xc = _ffn(xc, mods, l, 0, norm_g[l, 0], wg_all, wu_all, wd_all, ctx_row, in_place=not first)

        p_lat = _proj(xl, mods, l, norm_g[l, 1], w_in_all, lat_row(TM_FFN), rope=rope)
        p_ctx = _proj(xc, mods, l, norm_g[l, 1], w_in_all, ctx_row)

        sink_b = jnp.broadcast_to(attn_sink[l][:, None], (attn_sink.shape[1], LANES))
        ya_lat, ya_ctx = _attention(p_lat, p_ctx, sink_b, bsz, s_len, c_len, attn_w, off_k, need_ctx)

        r_ctx, h_end = _rglru(p_ctx.reshape(bsz, c_len, -1), off_xr, l, conv_w[l], conv_b[l], w_gates_all,
                              rg_b[l], rg_lambda[l], h_zero)
        r_lat, _ = _rglru(p_lat.reshape(bsz, s_len, -1), off_xr, l, conv_w[l], conv_b[l], w_gates_all,
                          rg_b[l], rg_lambda[l], h_end)

        xl = _merge(xl, ya_lat, r_lat.reshape(n_lat, rnn_w), p_lat, off_gr, off_ga, off_gb, mods, l,
                    woa_all, wor_all, wout_all, lat_row(TM_MERGE))
        if need_ctx:
            xc = _merge(xc, ya_ctx, r_ctx.reshape(n_ctx, rnn_w), p_ctx, off_gr, off_ga, off_gb, mods, l,
                        woa_all, wor_all, wout_all, ctx_row)
            xc = _ffn(xc, mods, l, 1, norm_g[l, 2], wg_all, wu_all, wd_all, ctx_row)
        xl = _ffn(xl, mods, l, 1, norm_g[l, 2], wg_all, wu_all, wd_all, lat_row(TM_FFN),
                  final_g=None if need_ctx else final_g)

    return xl.reshape(bsz, s_len, d)
```

```python
import functools

import jax
import jax.numpy as jnp
from jax import lax
from jax.experimental import pallas as pl
from jax.experimental.pallas import tpu as pltpu

F32 = jnp.float32
BF16 = jnp.bfloat16

LANES = 128
SUBLANES = 8
VMEM_PHYSICAL_BYTES = 64 * 1024 * 1024
VMEM_HEADROOM_BYTES = 4 * 1024 * 1024

HEAD_DIM = 128
N_KV_HEADS = 2
WINDOW_BLOCK = 128
GRID_W = 64
ROPE_BASE = 10000.0
ROPE_FREQS = HEAD_DIM // 4
CONV_LEFT = 2
RGLRU_C = 8.0
N_MOD = 9
EPS = 1e-6
MOD_ROWS = 16
NEG_BIG = -1e30
LOG2_E = 1.4426950408889634

TM_FFN = 1024
TF = 512
TN_PROJ = 1536
TM_MERGE = 512
TC_MERGE = 1024
TN_ADA = 1024
NORM_ROWS = 128
NORM_UNROLL = 4
SCAN_CHUNK = 256
SCAN_SUB = 32
SCAN_WIDTH = 512
ATTN_Q_BLOCKS = 2
CAST_ROWS = 256


def _cparams(sem, block_bytes):
    limit = min(int(block_bytes) + VMEM_HEADROOM_BYTES, VMEM_PHYSICAL_BYTES - VMEM_HEADROOM_BYTES // 2)
    return pltpu.CompilerParams(dimension_semantics=sem, vmem_limit_bytes=limit)


def _adaln_kernel(c_ref, w_ref, b_ref, o_ref):
    c = c_ref[...]
    sc = c * jax.nn.sigmoid(c)
    o_ref[...] = jnp.dot(sc, w_ref[...], preferred_element_type=F32,
                         precision=lax.Precision.HIGHEST) + b_ref[...]


def _adaln(c_all, w_ada, b_ada):
    depth, d, n = w_ada.shape
    return pl.pallas_call(
        _adaln_kernel,
        out_shape=jax.ShapeDtypeStruct((depth, MOD_ROWS, n), F32),
        grid=(depth, n // TN_ADA),
        in_specs=[
            pl.BlockSpec((MOD_ROWS, d), lambda l, j: (0, 0)),
            pl.BlockSpec((None, d, TN_ADA), lambda l, j: (l, 0, j)),
            pl.BlockSpec((None, 1, TN_ADA), lambda l, j: (l, 0, j)),
        ],
        out_specs=pl.BlockSpec((None, MOD_ROWS, TN_ADA), lambda l, j: (l, 0, j)),
        compiler_params=_cparams(("parallel", "parallel"), 2 * d * TN_ADA * 4 + 4 * MOD_ROWS * (d + TN_ADA) * 4),
        name="adaln",
    )(c_all, w_ada, b_ada.reshape(depth, 1, n))


def _rms_scale(x):
    return x * lax.rsqrt(jnp.mean(x * x, axis=-1, keepdims=True) + EPS)


def _norm_mod_store(xn_ref, rs_ref, x_ref, g_ref, sh_ref, sc_ref, zero_ref=None):
    n_steps = x_ref.shape[0] // NORM_ROWS
    inv_d = 1.0 / x_ref.shape[1]
    gs = g_ref[...] * (1.0 + sc_ref[...])
    sh = sh_ref[...]

    def stats(i, carry):
        r0 = pl.multiple_of(i * NORM_ROWS, NORM_ROWS)
        x = x_ref[pl.ds(r0, NORM_ROWS), :]
        rs_ref[pl.ds(r0, NORM_ROWS), :] = lax.rsqrt(jnp.sum(x * x, axis=-1, keepdims=True) * inv_d + EPS)
        return carry

    lax.fori_loop(0, n_steps, stats, 0, unroll=NORM_UNROLL)

    def scale(i, carry):
        r0 = pl.multiple_of(i * NORM_ROWS, NORM_ROWS)
        y = (x_ref[pl.ds(r0, NORM_ROWS), :] * rs_ref[pl.ds(r0, NORM_ROWS), :]) * gs + sh
        xn_ref[pl.ds(r0, NORM_ROWS), :] = y.astype(xn_ref.dtype)
        if zero_ref is not None:
            zero_ref[pl.ds(r0, NORM_ROWS), :] = jnp.zeros((NORM_ROWS, zero_ref.shape[1]), zero_ref.dtype)
        return carry

    lax.fori_loop(0, n_steps, scale, 0)


def _mod_spec(layer, k, row_of_tile, d):
    return pl.BlockSpec((None, None, 1, d), lambda i, j: (layer, row_of_tile(i), 0, k))


def _ffn_kernel(x_ref, g_ref, sh_ref, sc_ref, gt_ref, wg_ref, wu_ref, wd_ref, *refs, final_norm):
    if final_norm:
        fg_ref, o_ref, xn_ref, rs_ref = refs
    else:
        o_ref, xn_ref, rs_ref = refs
    c = pl.program_id(1)

    @pl.when(c == 0)
    def _():
        _norm_mod_store(xn_ref, rs_ref, x_ref, g_ref, sh_ref, sc_ref, zero_ref=o_ref)

    xn = xn_ref[...]
    h = jnp.dot(xn, wg_ref[...], preferred_element_type=F32)
    u = jnp.dot(xn, wu_ref[...], preferred_element_type=F32)
    a = ((h * jax.nn.sigmoid(h)) * u).astype(BF16)
    o_ref[...] += jnp.dot(a, wd_ref[...], preferred_element_type=F32)

    @pl.when(c == pl.num_programs(1) - 1)
    def _():
        gate = 0.5 * gt_ref[...]

        def body(i, carry):
            r0 = pl.multiple_of(i * NORM_ROWS, NORM_ROWS)
            y = x_ref[pl.ds(r0, NORM_ROWS), :] + gate * o_ref[pl.ds(r0, NORM_ROWS), :]
            if final_norm:
                y = _rms_scale(y) * fg_ref[...]
            o_ref[pl.ds(r0, NORM_ROWS), :] = y
            return carry

        lax.fori_loop(0, x_ref.shape[0] // NORM_ROWS, body, 0)


def _ffn(xs, mods, layer, half, g, wg, wu, wd, row_of_tile, final_g=None, in_place=True):
    rows, d = xs.shape
    ff = wg.shape[-1]
    tm = TM_FFN
    mod_k0 = 6 * half
    final_norm = final_g is not None
    in_specs = [
        pl.BlockSpec((tm, d), lambda i, c: (i, 0)),
        pl.BlockSpec((1, d), lambda i, c: (0, 0)),
        _mod_spec(layer, mod_k0, row_of_tile, d),
        _mod_spec(layer, mod_k0 + 1, row_of_tile, d),
        _mod_spec(layer, mod_k0 + 2, row_of_tile, d),
        pl.BlockSpec((None, None, d, TF), lambda i, c: (layer, half, 0, c)),
        pl.BlockSpec((None, None, d, TF), lambda i, c: (layer, half, 0, c)),
        pl.BlockSpec((None, None, TF, d), lambda i, c: (layer, half, c, 0)),
    ]
    args = [xs, g.reshape(1, d), mods, mods, mods, wg, wu, wd]
    if final_norm:
        in_specs.append(pl.BlockSpec((1, d), lambda i, c: (0, 0)))
        args.append(final_g.reshape(1, d))
    block_bytes = (2 * tm * d * 4 + 2 * tm * d * 4 + 2 * 3 * d * TF * 2 + tm * d * 2
                   + 2 * tm * TF * 4 + tm * TF * 2)
    return pl.pallas_call(
        functools.partial(_ffn_kernel, final_norm=final_norm),
        out_shape=jax.ShapeDtypeStruct((rows, d), F32),
        grid=(rows // tm, ff // TF),
        in_specs=in_specs,
        out_specs=pl.BlockSpec((tm, d), lambda i, c: (i, 0)),
        scratch_shapes=[pltpu.VMEM((tm, d), BF16), pltpu.VMEM((tm, 1), F32)],
        input_output_aliases={0: 0} if (in_place and not final_norm) else {},
        compiler_params=_cparams(("parallel", "arbitrary"), block_bytes),
        name="ffn_final" if final_norm else "ffn",
    )(*args)


def _rope(t, cs, sn):
    lane = lax.broadcasted_iota(jnp.int32, t.shape, 1)
    first_half = (lane & (2 * ROPE_FREQS - 1)) < ROPE_FREQS
    partner = jnp.where(first_half,
                        pltpu.roll(t, HEAD_DIM - ROPE_FREQS, axis=1),
                        pltpu.roll(t, ROPE_FREQS, axis=1))
    return t * cs + partner * sn


def _proj_kernel(x_ref, g_ref, sh_ref, sc_ref, w_ref, *refs, rope_cols, q_cols, q_scale):
    if rope_cols:
        cos_ref, sin_ref, o_ref, xn_ref, rs_ref = refs
    else:
        o_ref, xn_ref, rs_ref = refs
    j = pl.program_id(1)

    @pl.when(j == 0)
    def _():
        _norm_mod_store(xn_ref, rs_ref, x_ref, g_ref, sh_ref, sc_ref)

    def plain():
        o_ref[...] = jnp.dot(xn_ref[...], w_ref[...], preferred_element_type=F32).astype(o_ref.dtype)

    if not rope_cols:
        plain()
        return

    @pl.when(j == 0)
    def _():
        xn = xn_ref[...]
        cs = cos_ref[...]
        sn = sin_ref[...]
        pair = 2 * HEAD_DIM
        for c0 in range(0, o_ref.shape[1], pair):
            res = jnp.dot(xn, w_ref[:, c0:c0 + pair], preferred_element_type=F32)
            for h0 in range(0, pair, HEAD_DIM):
                t = res[:, h0:h0 + HEAD_DIM]
                if c0 + h0 < rope_cols:
                    t = _rope(t, cs, sn)
                if c0 + h0 < q_cols:
                    t = t * q_scale
                o_ref[:, c0 + h0:c0 + h0 + HEAD_DIM] = t.astype(o_ref.dtype)

    pl.when(j != 0)(plain)


def _proj(xs, mods, layer, g, w, row_of_tile, rope=None):
    rows, d = xs.shape
    n = w.shape[-1]
    tm = TM_FFN
    in_specs = [
        pl.BlockSpec((tm, d), lambda i, j: (i, 0)),
        pl.BlockSpec((1, d), lambda i, j: (0, 0)),
        _mod_spec(layer, 3, row_of_tile, d),
        _mod_spec(layer, 4, row_of_tile, d),
        pl.BlockSpec((None, d, TN_PROJ), lambda i, j: (layer, 0, j)),
    ]
    args = [xs, g.reshape(1, d), mods, mods, w]
    rope_cols = q_cols = 0
    q_scale = 1.0
    if rope is not None:
        cos_t, sin_t, rope_cols, q_cols, q_scale, tiles_per_sample = rope
        assert q_cols <= rope_cols <= TN_PROJ
        in_specs += [pl.BlockSpec((tm, HEAD_DIM), lambda i, j: (i % tiles_per_sample, 0))] * 2
        args += [cos_t, sin_t]
    block_bytes = (2 * tm * d * 4 + tm * d * 2 + 2 * d * TN_PROJ * 2 + 2 * tm * TN_PROJ * 2
                   + tm * TN_PROJ * 4 + 4 * tm * HEAD_DIM * 4)
    return pl.pallas_call(
        functools.partial(_proj_kernel, rope_cols=rope_cols, q_cols=q_cols, q_scale=q_scale),
        out_shape=jax.ShapeDtypeStruct((rows, n), BF16),
        grid=(rows // tm, n // TN_PROJ),
        in_specs=in_specs,
        out_specs=pl.BlockSpec((tm, TN_PROJ), lambda i, j: (i, j)),
        scratch_shapes=[pltpu.VMEM((tm, d), BF16), pltpu.VMEM((tm, 1), F32)],
        compiler_params=_cparams(("parallel", "arbitrary"), block_bytes),
        name="in_proj_rope" if rope_cols else "in_proj",
    )(*args)


def _softmax_pv(q4, kcat, vcat, sk, bias, scale):
    s = lax.dot_general(q4, kcat, (((1,), (1,)), ((), ())), preferred_element_type=F32)
    if scale is None:
        sk = sk * LOG2_E
        exp = jnp.exp2
    else:
        s = s * scale
        exp = jnp.exp
    if bias is not None:
        rb = bias.shape[0]
        s = jnp.concatenate([s[r0:r0 + rb] + bias for r0 in range(0, s.shape[0], rb)], axis=0)
    m = jnp.maximum(jnp.max(s, axis=-1, keepdims=True), sk)
    p = exp(s - m)
    denom = jnp.sum(p, axis=-1, keepdims=True) + exp(sk - m)
    o = jnp.dot(p.astype(BF16), vcat, preferred_element_type=F32)
    return o / denom


def _sink_rows(sink_ref, g, q_per_kv, rows):
    parts = [jnp.broadcast_to(sink_ref[g * q_per_kv + j:g * q_per_kv + j + 1, 0:1], (rows, 1))
             for j in range(q_per_kv)]
    return jnp.concatenate(parts, axis=0)


def _attn_lat_kernel(q_ref, kp_ref, kc_ref, kn_ref, vp_ref, vc_ref, vn_ref, kx_ref, vx_ref,
                     sink_ref, o_ref, *, nb, q_per_kv, scale):
    n = pl.program_id(1)
    blk = WINDOW_BLOCK
    dh = HEAD_DIM
    c_len = kx_ref.shape[0]
    nq = q_ref.shape[0] // blk
    keys = 3 * blk + c_len
    ri = lax.broadcasted_iota(jnp.int32, (blk, keys), 0)
    kj = lax.broadcasted_iota(jnp.int32, (blk, keys), 1)
    band = (kj >= 3 * blk) | ((kj >= ri) & (kj <= ri + 2 * blk))

    for qb in range(nq):
        rows = slice(qb * blk, (qb + 1) * blk)
        first = (n == 0) if qb == 0 else False
        last = (n == nb // nq - 1) if qb == nq - 1 else False
        lo = jnp.where(first, blk, 0)
        hi = jnp.where(last, 2 * blk, 3 * blk)
        bias = jnp.where(band & ((kj >= 3 * blk) | ((kj >= lo) & (kj < hi))), 0.0, NEG_BIG).astype(F32)

        def window(p_ref, c_ref, n_ref, x_ref, cols):
            parts = []
            for w in (qb - 1, qb, qb + 1):
                if w < 0:
                    parts.append(p_ref[:, cols])
                elif w >= nq:
                    parts.append(n_ref[:, cols])
                else:
                    parts.append(c_ref[w * blk:(w + 1) * blk, cols])
            return jnp.concatenate(parts + [x_ref[:, cols]], axis=0)

        for g in range(N_KV_HEADS):
            cols = slice(g * dh, (g + 1) * dh)
            kcat = window(kp_ref, kc_ref, kn_ref, kx_ref, cols)
            vcat = window(vp_ref, vc_ref, vn_ref, vx_ref, cols)
            q4 = jnp.concatenate([q_ref[rows, (g * q_per_kv + j) * dh:(g * q_per_kv + j + 1) * dh]
                                  for j in range(q_per_kv)], axis=0)
            sk = _sink_rows(sink_ref, g, q_per_kv, blk)
            o = _softmax_pv(q4, kcat, vcat, sk, bias, scale)
            for j in range(q_per_kv):
                h = g * q_per_kv + j
                o_ref[rows, h * dh:(h + 1) * dh] = o[j * blk:(j + 1) * blk].astype(o_ref.dtype)


def _attn_ctx_kernel(q_ref, kx_ref, vx_ref, sink_ref, o_ref, *, q_per_kv, scale):
    dh = HEAD_DIM
    c_len = q_ref.shape[0]
    for g in range(N_KV_HEADS):
        cols = slice(g * dh, (g + 1) * dh)
        q4 = jnp.concatenate([q_ref[:, (g * q_per_kv + j) * dh:(g * q_per_kv + j + 1) * dh]
                              for j in range(q_per_kv)], axis=0)
        sk = _sink_rows(sink_ref, g, q_per_kv, c_len)
        o = _softmax_pv(q4, kx_ref[:, cols], vx_ref[:, cols], sk, None, scale)
        for j in range(q_per_kv):
            h = g * q_per_kv + j
            o_ref[:, h * dh:(h + 1) * dh] = o[j * c_len:(j + 1) * c_len].astype(o_ref.dtype)


def _attention(p_lat, p_ctx, sink_b, bsz, s_len, c_len, attn_w, k_off, need_ctx):
    kv_w = N_KV_HEADS * HEAD_DIM
    n_q_heads = attn_w // HEAD_DIM
    q_per_kv = n_q_heads // N_KV_HEADS
    blk = WINDOW_BLOCK
    nb = s_len // blk
    k_col = k_off // kv_w
    v_col = k_col + 1
    scale = HEAD_DIM ** -0.5

    nq = ATTN_Q_BLOCKS
    steps = nb // nq
    assert nb % nq == 0

    def kv_specs(col):
        prev = pl.BlockSpec((blk, kv_w), lambda b, n: (b * nb + jnp.maximum(n * nq - 1, 0), col))
        own = pl.BlockSpec((nq * blk, kv_w), lambda b, n: (b * steps + n, col))
        nxt = pl.BlockSpec((blk, kv_w), lambda b, n: (b * nb + jnp.minimum((n + 1) * nq, nb - 1), col))
        return [prev, own, nxt]

    ya_lat = pl.pallas_call(
        functools.partial(_attn_lat_kernel, nb=nb, q_per_kv=q_per_kv, scale=None),
        out_shape=jax.ShapeDtypeStruct((bsz * s_len, attn_w), BF16),
        grid=(bsz, steps),
        in_specs=[pl.BlockSpec((nq * blk, attn_w), lambda b, n: (b * steps + n, 0))]
        + kv_specs(k_col) + kv_specs(v_col) + [
            pl.BlockSpec((c_len, kv_w), lambda b, n: (b, k_col)),
            pl.BlockSpec((c_len, kv_w), lambda b, n: (b, v_col)),
            pl.BlockSpec((n_q_heads, LANES), lambda b, n: (0, 0)),
        ],
        out_specs=pl.BlockSpec((nq * blk, attn_w), lambda b, n: (b * steps + n, 0)),
        compiler_params=_cparams(("parallel", "parallel"), 16 * 1024 * 1024),
        name="attn_latent",
    )(p_lat, p_lat, p_lat, p_lat, p_lat, p_lat, p_lat, p_ctx, p_ctx, sink_b)
    if not need_ctx:
        return ya_lat, None
    ya_ctx = pl.pallas_call(
        functools.partial(_attn_ctx_kernel, q_per_kv=q_per_kv, scale=scale),
        out_shape=jax.ShapeDtypeStruct((bsz * c_len, attn_w), BF16),
        grid=(bsz,),
        in_specs=[
            pl.BlockSpec((c_len, attn_w), lambda b: (b, 0)),
            pl.BlockSpec((c_len, kv_w), lambda b: (b, k_col)),
            pl.BlockSpec((c_len, kv_w), lambda b: (b, v_col)),
            pl.BlockSpec((n_q_heads, LANES), lambda b: (0, 0)),
        ],
        out_specs=pl.BlockSpec((c_len, attn_w), lambda b: (b, 0)),
        compiler_params=_cparams(("parallel",), 16 * 1024 * 1024),
        name="attn_context",
    )(p_ctx, p_ctx, p_ctx, sink_b)
    return ya_lat, ya_ctx


def _rglru_kernel(xp_ref, x_ref, xn_ref, cw_ref, cb_ref, w_ref, b_ref, lam_ref, h0_ref, *refs,
                  reverse, nk):
    if reverse:
        hf_ref, y_ref, hend_ref, xt_ref, hs_ref, h_ref = refs
    else:
        y_ref, hend_ref, xt_ref, hs_ref, h_ref = refs
    k = pl.program_id(1)
    ci = nk - 1 - k if reverse else k
    bsz, tc, width = x_ref.shape
    n_lb = width // LANES
    halo = xp_ref.shape[1]
    n_sub = tc // SCAN_SUB
    rows_sub = SCAN_SUB * bsz

    @pl.when(k == 0)
    def _():
        h_ref[...] = h0_ref[...]

    for c in range(n_lb):
        lanes = slice(c * LANES, (c + 1) * LANES)
        for b in range(bsz):
            xt_ref[c, pl.ds(b, halo, stride=bsz), :] = xp_ref[b, :, lanes].astype(F32)
            xt_ref[c, pl.ds(halo * bsz + b, tc, stride=bsz), :] = x_ref[b, :, lanes].astype(F32)
            xt_ref[c, pl.ds((halo + tc) * bsz + b, halo, stride=bsz), :] = xn_ref[b, :, lanes].astype(F32)

    @pl.when(ci == 0)
    def _():
        xt_ref[:, 0:halo * bsz, :] = jnp.zeros((n_lb, halo * bsz, LANES), F32)

    @pl.when(ci == nk - 1)
    def _():
        xt_ref[:, (halo + tc) * bsz:(2 * halo + tc) * bsz, :] = jnp.zeros((n_lb, halo * bsz, LANES), F32)

    def sub_body(i, hs):
        si = (n_sub - 1 - i) if reverse else i
        t0 = si * SCAN_SUB
        new_h = []
        for c in range(n_lb):
            lanes = slice(c * LANES, (c + 1) * LANES)
            cw = cw_ref[:, lanes]
            u = cb_ref[:, lanes]
            for kk in range(cw.shape[0]):
                r0 = pl.multiple_of((t0 + halo - CONV_LEFT + kk) * bsz, bsz)
                u = u + cw[kk:kk + 1, :] * xt_ref[c, pl.ds(r0, rows_sub), :]
            g = jnp.dot(u.astype(BF16), w_ref[c], preferred_element_type=F32)
            gr = jax.nn.sigmoid(g[:, :LANES] + b_ref[0:1, lanes])
            gi = jax.nn.sigmoid(g[:, LANES:] + b_ref[1:2, lanes])
            neg = -lam_ref[:, lanes]
            softplus = jnp.maximum(neg, 0.0) + jnp.log1p(jnp.exp(-jnp.abs(neg)))
            log_a = gr * ((-RGLRU_C) * softplus)
            a = jnp.exp(log_a)
            x1 = -jnp.tanh(log_a) * (a * a + 1.0)
            m = jnp.where(x1 > 0.0, x1 * lax.rsqrt(x1), 0.0)
            bx = m * (gi * u)
            h = hs[c]
            outs = [None] * SCAN_SUB
            for t in (range(SCAN_SUB - 1, -1, -1) if reverse else range(SCAN_SUB)):
                h = a[t * bsz:(t + 1) * bsz] * h + bx[t * bsz:(t + 1) * bsz]
                outs[t] = h
            hs_ref[c, pl.ds(pl.multiple_of(t0 * bsz, rows_sub), rows_sub), :] = jnp.concatenate(outs, axis=0)
            new_h.append(h)
        return tuple(new_h)

    h_init = tuple(h_ref[:, c * LANES:(c + 1) * LANES] for c in range(n_lb))
    h_fin = lax.fori_loop(0, n_sub, sub_body, h_init)
    for c in range(n_lb):
        lanes = slice(c * LANES, (c + 1) * LANES)
        h_ref[:, lanes] = h_fin[c]
        for b in range(bsz):
            yb = hs_ref[c, pl.ds(b, tc, stride=bsz), :]
            if reverse:
                yb = yb + hf_ref[b, :, lanes]
            y_ref[b, :, lanes] = yb.astype(y_ref.dtype)

    @pl.when(k == nk - 1)
    def _():
        hend_ref[...] = h_ref[...]


def _rglru_dir(p3, col0, layer, conv_w, conv_b, w_gates, b_gates, lam, h0, hf, reverse):
    direction = 1 if reverse else 0
    bsz, seq, _ = p3.shape
    rnn_w = conv_w.shape[-1]
    width = SCAN_WIDTH
    tc = min(SCAN_CHUNK, seq)
    nk = seq // tc
    n_lb = width // LANES
    cblk0 = col0 // width
    halo = 2 * SUBLANES
    hpt = tc // halo
    n_halo_blocks = seq // halo

    def tmap(k):
        return nk - 1 - k if reverse else k

    in_specs = [
        pl.BlockSpec((bsz, halo, width), lambda j, k: (0, jnp.maximum(tmap(k) * hpt - 1, 0), cblk0 + j)),
        pl.BlockSpec((bsz, tc, width), lambda j, k: (0, tmap(k), cblk0 + j)),
        pl.BlockSpec((bsz, halo, width),
                     lambda j, k: (0, jnp.minimum((tmap(k) + 1) * hpt, n_halo_blocks - 1), cblk0 + j)),
        pl.BlockSpec((conv_w.shape[0], width), lambda j, k: (0, j)),
        pl.BlockSpec((1, width), lambda j, k: (0, j)),
        pl.BlockSpec((None, None, n_lb, LANES, 2 * LANES), lambda j, k: (layer, direction, j, 0, 0)),
        pl.BlockSpec((2, width), lambda j, k: (0, j)),
        pl.BlockSpec((1, width), lambda j, k: (0, j)),
        pl.BlockSpec((bsz, width), lambda j, k: (0, j)),
    ]
    args = [p3, p3, p3, conv_w, conv_b.reshape(1, rnn_w), w_gates, b_gates, lam.reshape(1, rnn_w), h0]
    if reverse:
        in_specs.append(pl.BlockSpec((bsz, tc, width), lambda j, k: (0, tmap(k), j)))
        args.append(hf)
    out_dtype = BF16 if reverse else F32
    block_bytes = (2 * bsz * tc * width * 2 + 2 * bsz * tc * width * 4 * 2
                   + (tc + 2 * halo) * bsz * width * 4 + tc * bsz * width * 4)
    return pl.pallas_call(
        functools.partial(_rglru_kernel, reverse=reverse, nk=nk),
        out_shape=(jax.ShapeDtypeStruct((bsz, seq, rnn_w), out_dtype),
                   jax.ShapeDtypeStruct((bsz, rnn_w), F32)),
        grid=(rnn_w // width, nk),
        in_specs=in_specs,
        out_specs=(pl.BlockSpec((bsz, tc, width), lambda j, k: (0, tmap(k), j)),
                   pl.BlockSpec((bsz, width), lambda j, k: (0, j))),
        scratch_shapes=[pltpu.VMEM((n_lb, (tc + 2 * halo) * bsz, LANES), F32),
                        pltpu.VMEM((n_lb, tc * bsz, LANES), F32),
                        pltpu.VMEM((bsz, width), F32)],
        compiler_params=_cparams(("parallel", "arbitrary"), block_bytes),
        name="rglru_bwd" if reverse else "rglru_fwd",
    )(*args)


def _rglru(p3, col0, layer, conv_w, conv_b, w_gates, b_gates, lam, h0):
    hf, end_f = _rglru_dir(p3, col0, layer, conv_w, conv_b, w_gates, b_gates[0], lam[0], h0[0], None, False)
    y, end_b = _rglru_dir(p3, col0, layer, conv_w, conv_b, w_gates, b_gates[1], lam[1], h0[1], hf, True)
    return y, (end_f, end_b)


def _gelu_tanh(x):
    return 0.5 * x * (1.0 + jnp.tanh(0.7978845608028654 * (x + 0.044715 * (x * x * x))))


def _merge_kernel(x_ref, ya_ref, r_ref, gr0_ref, gr1_ref, ga0_ref, ga1_ref, gb0_ref, gb1_ref, gt_ref,
                  woa_ref, wor_ref, wout_ref, o_ref, yr_ref, acc_ref):
    c = pl.program_id(1)

    def wide(lo_ref, hi_ref):
        return jnp.concatenate([lo_ref[...], hi_ref[...]], axis=1).astype(F32)

    @pl.when(c == 0)
    def _():
        yr_ref[...] = (r_ref[...].astype(F32) * _gelu_tanh(wide(gr0_ref, gr1_ref))).astype(yr_ref.dtype)
        acc_ref[...] = jnp.zeros_like(acc_ref)

    pa = jnp.dot(ya_ref[...], woa_ref[...], preferred_element_type=F32)
    pr = jnp.dot(yr_ref[...], wor_ref[...], preferred_element_type=F32)
    t = jax.nn.sigmoid(wide(ga0_ref, ga1_ref)) * pa + jax.nn.sigmoid(wide(gb0_ref, gb1_ref)) * pr
    acc_ref[...] += jnp.dot(t.astype(BF16), wout_ref[...], preferred_element_type=F32)

    @pl.when(c == pl.num_programs(1) - 1)
    def _():
        o_ref[...] = x_ref[...] + gt_ref[...] * acc_ref[...]


def _merge(xs, ya, r, proj, gr_off, ga_off, gb_off, mods, layer, woa, wor, wout, row_of_tile):
    rows, d = xs.shape
    attn_w = ya.shape[1]
    rnn_w = r.shape[1]
    tm, tc = TM_MERGE, TC_MERGE
    half = tc // 2
    assert rnn_w == tc and gr_off % half == 0 and ga_off % half == 0 and gb_off % half == 0

    def half_spec(off, k):
        return pl.BlockSpec((tm, half), lambda i, c: (i, off // half + 2 * c + k))

    def gr_spec(k):
        return pl.BlockSpec((tm, half), lambda i, c: (i, gr_off // half + k))

    block_bytes = (4 * tm * d * 4 + 2 * tm * (attn_w + 2 * rnn_w + 2 * tc) * 2
                   + 2 * (attn_w + rnn_w + d) * tc * 2 + tm * rnn_w * 2 + tm * d * 4 + 3 * tm * tc * 4)
    return pl.pallas_call(
        _merge_kernel,
        out_shape=jax.ShapeDtypeStruct((rows, d), F32),
        grid=(rows // tm, d // tc),
        in_specs=[
            pl.BlockSpec((tm, d), lambda i, c: (i, 0)),
            pl.BlockSpec((tm, attn_w), lambda i, c: (i, 0)),
            pl.BlockSpec((tm, rnn_w), lambda i, c: (i, 0)),
            gr_spec(0), gr_spec(1),
            half_spec(ga_off, 0), half_spec(ga_off, 1),
            half_spec(gb_off, 0), half_spec(gb_off, 1),
            _mod_spec(layer, 5, row_of_tile, d),
            pl.BlockSpec((None, attn_w, tc), lambda i, c: (layer, 0, c)),
            pl.BlockSpec((None, rnn_w, tc), lambda i, c: (layer, 0, c)),
            pl.BlockSpec((None, tc, d), lambda i, c: (layer, c, 0)),
        ],
        out_specs=pl.BlockSpec((tm, d), lambda i, c: (i, 0)),
        scratch_shapes=[pltpu.VMEM((tm, rnn_w), BF16), pltpu.VMEM((tm, d), F32)],
        input_output_aliases={0: 0},
        compiler_params=_cparams(("parallel", "arbitrary"), block_bytes),
        name="merge",
    )(xs, ya, r, proj, proj, proj, proj, proj, proj, mods, woa, wor, wout)


def _up_cast_kernel(w_ref, g_ref, u_ref, *, d_ff):
    pad = g_ref.shape[1] - d_ff
    g_ref[:, :d_ff] = w_ref[:, :d_ff].astype(g_ref.dtype)
    u_ref[:, :d_ff] = w_ref[:, d_ff:].astype(u_ref.dtype)
    if pad:
        g_ref[:, d_ff:] = jnp.zeros((g_ref.shape[0], pad), g_ref.dtype)
        u_ref[:, d_ff:] = jnp.zeros((u_ref.shape[0], pad), u_ref.dtype)


def _ffn_up_weights(w_up, d_ff, ff_pad):
    depth, halves, d, n = w_up.shape
    assert n == 2 * d_ff and d_ff % LANES == 0 and d % CAST_ROWS == 0
    out = jax.ShapeDtypeStruct((depth * halves, d, ff_pad), BF16)
    wg, wu = pl.pallas_call(
        functools.partial(_up_cast_kernel, d_ff=d_ff),
        out_shape=(out, out),
        grid=(depth * halves, d // CAST_ROWS),
        in_specs=[pl.BlockSpec((None, CAST_ROWS, n), lambda l, r: (l, r, 0))],
        out_specs=(pl.BlockSpec((None, CAST_ROWS, ff_pad), lambda l, r: (l, r, 0)),
                   pl.BlockSpec((None, CAST_ROWS, ff_pad), lambda l, r: (l, r, 0))),
        compiler_params=_cparams(("parallel", "parallel"), 2 * CAST_ROWS * (n * 4 + 2 * ff_pad * 2)),
        name="ffn_up_cast",
    )(w_up.reshape(depth * halves, d, n))
    return wg.reshape(depth, halves, d, ff_pad), wu.reshape(depth, halves, d, ff_pad)


def _down_cast_kernel(w_ref, o_ref, *, tail_rows):
    r = pl.program_id(1)
    last = pl.num_programs(1) - 1

    @pl.when(r != last)
    def _():
        o_ref[...] = w_ref[...].astype(o_ref.dtype)

    @pl.when(r == last)
    def _():
        o_ref[:tail_rows, :] = w_ref[:tail_rows, :].astype(o_ref.dtype)
        if tail_rows < o_ref.shape[0]:
            o_ref[tail_rows:, :] = jnp.zeros((o_ref.shape[0] - tail_rows, o_ref.shape[1]), o_ref.dtype)


def _ffn_down_weights(w_down, ff_pad):
    depth, halves, d_ff, d = w_down.shape
    n_blocks = ff_pad // TF
    tail_rows = d_ff - (n_blocks - 1) * TF
    assert 0 < tail_rows <= TF and tail_rows % (2 * SUBLANES) == 0
    wd = pl.pallas_call(
        functools.partial(_down_cast_kernel, tail_rows=tail_rows),
        out_shape=jax.ShapeDtypeStruct((depth * halves, ff_pad, d), BF16),
        grid=(depth * halves, n_blocks),
        in_specs=[pl.BlockSpec((None, TF, d), lambda l, r: (l, r, 0))],
        out_specs=pl.BlockSpec((None, TF, d), lambda l, r: (l, r, 0)),
        compiler_params=_cparams(("parallel", "parallel"), 2 * TF * d * (4 + 2)),
        name="ffn_down_cast",
    )(w_down.reshape(depth * halves, d_ff, d))
    return wd.reshape(depth, halves, ff_pad, d)


def _rope_tables(s_len):
    pos = jnp.arange(s_len, dtype=jnp.int32)
    row = (pos // GRID_W).astype(F32)
    col = (pos % GRID_W).astype(F32)
    inv_freq = ROPE_BASE ** (-jnp.arange(ROPE_FREQS, dtype=F32) / ROPE_FREQS)
    ang_r = row[:, None] * inv_freq
    ang_c = col[:, None] * inv_freq
    cos_t = jnp.concatenate([jnp.cos(ang_r), jnp.cos(ang_r), jnp.cos(ang_c), jnp.cos(ang_c)], axis=1)
    sin_t = jnp.concatenate([-jnp.sin(ang_r), jnp.sin(ang_r), -jnp.sin(ang_c), jnp.sin(ang_c)], axis=1)
    return cos_t, sin_t


def kernel(x, c, ctx, c_ctx, w_ada, b_ada, norm_g, final_g, w_ffn_up, w_ffn_down, w_in, attn_sink,
           conv_w, conv_b, rg_w, rg_b, rg_lambda, w_o_attn, w_o_rnn, w_out):
    bsz, s_len, d = x.shape
    c_len = ctx.shape[1]
    depth = w_ada.shape[0]
    d_ff = w_ffn_down.shape[2]
    attn_w = w_o_attn.shape[1]
    rnn_w = w_o_rnn.shape[1]
    kv_w = N_KV_HEADS * HEAD_DIM
    ff_pad = -(-d_ff // TF) * TF
    n_lat, n_ctx = bsz * s_len, bsz * c_len
    assert s_len % TM_FFN == 0 and n_ctx % TM_FFN == 0 and s_len % TM_MERGE == 0 and n_ctx % TM_MERGE == 0
    assert bsz == SUBLANES and bsz + 1 <= MOD_ROWS and rnn_w % SCAN_WIDTH == 0
    assert s_len % SCAN_CHUNK == 0 and c_len % SCAN_SUB == 0 and c_len <= SCAN_CHUNK

    off_k = attn_w
    off_xr = off_k + 2 * kv_w
    off_gr = off_xr + rnn_w
    off_ga = off_gr + rnn_w
    off_gb = off_ga + d
    assert off_xr == TN_PROJ and off_k % kv_w == 0 and off_xr % SCAN_WIDTH == 0

    def lat_row(tm):
        return lambda i: i // (s_len // tm)

    def ctx_row(i):
        return bsz

    wg_all, wu_all = _ffn_up_weights(w_ffn_up, d_ff, ff_pad)
    wd_all = _ffn_down_weights(w_ffn_down, ff_pad)
    w_in_all = w_in.astype(BF16)
    woa_all, wor_all, wout_all = w_o_attn.astype(BF16), w_o_rnn.astype(BF16), w_out.astype(BF16)
    w_gates_all = jnp.concatenate([rg_w[:, :, 0], rg_w[:, :, 1]], axis=-1).astype(BF16)

    c_all = jnp.zeros((MOD_ROWS, d), F32).at[:bsz].set(c).at[bsz].set(c_ctx)
    mods = _adaln(c_all, w_ada, b_ada).reshape(depth, MOD_ROWS, 1, N_MOD * d)
    cos_t, sin_t = _rope_tables(s_len)
    rope = (cos_t, sin_t, attn_w + kv_w, attn_w, HEAD_DIM ** -0.5 * LOG2_E, s_len // TM_FFN)
    xl = x.reshape(n_lat, d)
    xc = ctx.reshape(n_ctx, d)
    h_zero = (jnp.zeros((bsz, rnn_w), F32), jnp.zeros((bsz, rnn_w), F32))

    for l in range(depth):
        need_ctx = l < depth - 1
        first = l == 0
        xl = _ffn(xl, mods, l, 0, norm_g[l, 0], wg_all, wu_all, wd_all, lat_row(TM_FFN), in_place=not first)
        xc = _ffn(xc, mods, l, 0, norm_g[l, 0], wg_all, wu_all, wd_all, ctx_row, in_place=not first)

        p_lat = _proj(xl, mods, l, norm_g[l, 1], w_in_all, lat_row(TM_FFN), rope=rope)
        p_ctx = _proj(xc, mods, l, norm_g[l, 1], w_in_all, ctx_row)

        sink_b = jnp.broadcast_to(attn_sink[l][:, None], (attn_sink.shape[1], LANES))
        ya_lat, ya_ctx = _attention(p_lat, p_ctx, sink_b, bsz, s_len, c_len, attn_w, off_k, need_ctx)

        r_ctx, h_end = _rglru(p_ctx.reshape(bsz, c_len, -1), off_xr, l, conv_w[l], conv_b[l], w_gates_all,
                              rg_b[l], rg_lambda[l], h_zero)
        r_lat, _ = _rglru(p_lat.reshape(bsz, s_len, -1), off_xr, l, conv_w[l], conv_b[l], w_gates_all,
                          rg_b[l], rg_lambda[l], h_end)

        xl = _merge(xl, ya_lat, r_lat.reshape(n_lat, rnn_w), p_lat, off_gr, off_ga, off_gb, mods, l,
                    woa_all, wor_all, wout_all, lat_row(TM_MERGE))
        if need_ctx:
            xc = _merge(xc, ya_ctx, r_ctx.reshape(n_ctx, rnn_w), p_ctx, off_gr, off_ga, off_gb, mods, l,
                        woa_all, wor_all, wout_all, ctx_row)
            xc = _ffn(xc, mods, l, 1, norm_g[l, 2], wg_all, wu_all, wd_all, ctx_row)
        xl = _ffn(xl, mods, l, 1, norm_g[l, 2], wg_all, wu_all, wd_all, lat_row(TM_FFN),
                  final_g=None if need_ctx else final_g)

    return xl.reshape(bsz, s_len, d)
```

```python
import functools

import jax
import jax.numpy as jnp
from jax import lax
from jax.experimental import pallas as pl
from jax.experimental.pallas import tpu as pltpu

F32 = jnp.float32
BF16 = jnp.bfloat16

LANES = 128
SUBLANES = 8
VMEM_PHYSICAL_BYTES = 64 * 1024 * 1024
VMEM_HEADROOM_BYTES = 4 * 1024 * 1024

HEAD_DIM = 128
N_KV_HEADS = 2
WINDOW_BLOCK = 128
GRID_W = 64
ROPE_BASE = 10000.0
ROPE_FREQS = HEAD_DIM // 4
CONV_LEFT = 2
RGLRU_C = 8.0
N_MOD = 9
EPS = 1e-6
MOD_ROWS = 16
NEG_BIG = -1e30
LOG2_E = 1.4426950408889634

TM_FFN = 1024
TF = 512
TN_PROJ = 1536
TM_MERGE = 512
TC_MERGE = 1024
TN_ADA = 1024
NORM_ROWS = 128
NORM_UNROLL = 4
SCAN_CHUNK = 256
SCAN_SUB = 32
SCAN_WIDTH = 512
ATTN_Q_BLOCKS = 8
CAST_ROWS = 256


def _cparams(sem, block_bytes):
    limit = min(int(block_bytes) + VMEM_HEADROOM_BYTES, VMEM_PHYSICAL_BYTES - VMEM_HEADROOM_BYTES // 2)
    return pltpu.CompilerParams(dimension_semantics=sem, vmem_limit_bytes=limit)


def _adaln_kernel(c_ref, w_ref, b_ref, o_ref):
    c = c_ref[...]
    sc = c * jax.nn.sigmoid(c)
    o_ref[...] = jnp.dot(sc, w_ref[...], preferred_element_type=F32,
                         precision=lax.Precision.HIGHEST) + b_ref[...]


def _adaln(c_all, w_ada, b_ada):
    depth, d, n = w_ada.shape
    return pl.pallas_call(
        _adaln_kernel,
        out_shape=jax.ShapeDtypeStruct((depth, MOD_ROWS, n), F32),
        grid=(depth, n // TN_ADA),
        in_specs=[
            pl.BlockSpec((MOD_ROWS, d), lambda l, j: (0, 0)),
            pl.BlockSpec((None, d, TN_ADA), lambda l, j: (l, 0, j)),
            pl.BlockSpec((None, 1, TN_ADA), lambda l, j: (l, 0, j)),
        ],
        out_specs=pl.BlockSpec((None, MOD_ROWS, TN_ADA), lambda l, j: (l, 0, j)),
        compiler_params=_cparams(("parallel", "parallel"), 2 * d * TN_ADA * 4 + 4 * MOD_ROWS * (d + TN_ADA) * 4),
        name="adaln",
    )(c_all, w_ada, b_ada.reshape(depth, 1, n))


def _rms_scale(x):
    return x * lax.rsqrt(jnp.mean(x * x, axis=-1, keepdims=True) + EPS)


def _norm_mod_store(xn_ref, rs_ref, x_ref, g_ref, sh_ref, sc_ref, zero_ref=None):
    n_steps = x_ref.shape[0] // NORM_ROWS
    inv_d = 1.0 / x_ref.shape[1]
    gs = g_ref[...] * (1.0 + sc_ref[...])
    sh = sh_ref[...]

    def stats(i, carry):
        r0 = pl.multiple_of(i * NORM_ROWS, NORM_ROWS)
        x = x_ref[pl.ds(r0, NORM_ROWS), :]
        rs_ref[pl.ds(r0, NORM_ROWS), :] = lax.rsqrt(jnp.sum(x * x, axis=-1, keepdims=True) * inv_d + EPS)
        return carry

    lax.fori_loop(0, n_steps, stats, 0, unroll=NORM_UNROLL)

    def scale(i, carry):
        r0 = pl.multiple_of(i * NORM_ROWS, NORM_ROWS)
        y = (x_ref[pl.ds(r0, NORM_ROWS), :] * rs_ref[pl.ds(r0, NORM_ROWS), :]) * gs + sh
        xn_ref[pl.ds(r0, NORM_ROWS), :] = y.astype(xn_ref.dtype)
        if zero_ref is not None:
            zero_ref[pl.ds(r0, NORM_ROWS), :] = jnp.zeros((NORM_ROWS, zero_ref.shape[1]), zero_ref.dtype)
        return carry

    lax.fori_loop(0, n_steps, scale, 0)


def _mod_spec(layer, k, row_of_tile, d):
    return pl.BlockSpec((None, None, 1, d), lambda i, j: (layer, row_of_tile(i), 0, k))


def _ffn_kernel(x_ref, g_ref, sh_ref, sc_ref, gt_ref, wg_ref, wu_ref, wd_ref, *refs, final_norm):
    if final_norm:
        fg_ref, o_ref, xn_ref, rs_ref = refs
    else:
        o_ref, xn_ref, rs_ref = refs
    c = pl.program_id(1)

    @pl.when(c == 0)
    def _():
        _norm_mod_store(xn_ref, rs_ref, x_ref, g_ref, sh_ref, sc_ref, zero_ref=o_ref)

    xn = xn_ref[...]
    h = jnp.dot(xn, wg_ref[...], preferred_element_type=F32)
    u = jnp.dot(xn, wu_ref[...], preferred_element_type=F32)
    a = ((h * jax.nn.sigmoid(h)) * u).astype(BF16)
    o_ref[...] += jnp.dot(a, wd_ref[...], preferred_element_type=F32)

    @pl.when(c == pl.num_programs(1) - 1)
    def _():
        gate = 0.5 * gt_ref[...]

        def body(i, carry):
            r0 = pl.multiple_of(i * NORM_ROWS, NORM_ROWS)
            y = x_ref[pl.ds(r0, NORM_ROWS), :] + gate * o_ref[pl.ds(r0, NORM_ROWS), :]
            if final_norm:
                y = _rms_scale(y) * fg_ref[...]
            o_ref[pl.ds(r0, NORM_ROWS), :] = y
            return carry

        lax.fori_loop(0, x_ref.shape[0] // NORM_ROWS, body, 0)


def _ffn(xs, mods, layer, half, g, wg, wu, wd, row_of_tile, final_g=None, in_place=True):
    rows, d = xs.shape
    ff = wg.shape[-1]
    tm = TM_FFN
    mod_k0 = 6 * half
    final_norm = final_g is not None
    in_specs = [
        pl.BlockSpec((tm, d), lambda i, c: (i, 0)),
        pl.BlockSpec((1, d), lambda i, c: (0, 0)),
        _mod_spec(layer, mod_k0, row_of_tile, d),
        _mod_spec(layer, mod_k0 + 1, row_of_tile, d),
        _mod_spec(layer, mod_k0 + 2, row_of_tile, d),
        pl.BlockSpec((None, None, d, TF), lambda i, c: (layer, half, 0, c)),
        pl.BlockSpec((None, None, d, TF), lambda i, c: (layer, half, 0, c)),
        pl.BlockSpec((None, None, TF, d), lambda i, c: (layer, half, c, 0)),
    ]
    args = [xs, g.reshape(1, d), mods, mods, mods, wg, wu, wd]
    if final_norm:
        in_specs.append(pl.BlockSpec((1, d), lambda i, c: (0, 0)))
        args.append(final_g.reshape(1, d))
    block_bytes = (2 * tm * d * 4 + 2 * tm * d * 4 + 2 * 3 * d * TF * 2 + tm * d * 2
                   + 2 * tm * TF * 4 + tm * TF * 2)
    return pl.pallas_call(
        functools.partial(_ffn_kernel, final_norm=final_norm),
        out_shape=jax.ShapeDtypeStruct((rows, d), F32),
        grid=(rows // tm, ff // TF),
        in_specs=in_specs,
        out_specs=pl.BlockSpec((tm, d), lambda i, c: (i, 0)),
        scratch_shapes=[pltpu.VMEM((tm, d), BF16), pltpu.VMEM((tm, 1), F32)],
        input_output_aliases={0: 0} if (in_place and not final_norm) else {},
        compiler_params=_cparams(("parallel", "arbitrary"), block_bytes),
        name="ffn_final" if final_norm else "ffn",
    )(*args)


def _rope(t, cs, sn):
    lane = lax.broadcasted_iota(jnp.int32, t.shape, 1)
    first_half = (lane & (2 * ROPE_FREQS - 1)) < ROPE_FREQS
    partner = jnp.where(first_half,
                        pltpu.roll(t, HEAD_DIM - ROPE_FREQS, axis=1),
                        pltpu.roll(t, ROPE_FREQS, axis=1))
    return t * cs + partner * sn


def _proj_kernel(x_ref, g_ref, sh_ref, sc_ref, w_ref, *refs, rope_cols, q_cols, q_scale):
    if rope_cols:
        cos_ref, sin_ref, o_ref, xn_ref, rs_ref = refs
    else:
        o_ref, xn_ref, rs_ref = refs
    j = pl.program_id(1)

    @pl.when(j == 0)
    def _():
        _norm_mod_store(xn_ref, rs_ref, x_ref, g_ref, sh_ref, sc_ref)

    def plain():
        o_ref[...] = jnp.dot(xn_ref[...], w_ref[...], preferred_element_type=F32).astype(o_ref.dtype)

    if not rope_cols:
        plain()
        return

    @pl.when(j == 0)
    def _():
        xn = xn_ref[...]
        cs = cos_ref[...]
        sn = sin_ref[...]
        pair = 2 * HEAD_DIM
        for c0 in range(0, o_ref.shape[1], pair):
            res = jnp.dot(xn, w_ref[:, c0:c0 + pair], preferred_element_type=F32)
            for h0 in range(0, pair, HEAD_DIM):
                t = res[:, h0:h0 + HEAD_DIM]
                if c0 + h0 < rope_cols:
                    t = _rope(t, cs, sn)
                if c0 + h0 < q_cols:
                    t = t * q_scale
                o_ref[:, c0 + h0:c0 + h0 + HEAD_DIM] = t.astype(o_ref.dtype)

    pl.when(j != 0)(plain)


def _proj(xs, mods, layer, g, w, row_of_tile, rope=None):
    rows, d = xs.shape
    n = w.shape[-1]
    tm = TM_FFN
    in_specs = [
        pl.BlockSpec((tm, d), lambda i, j: (i, 0)),
        pl.BlockSpec((1, d), lambda i, j: (0, 0)),
        _mod_spec(layer, 3, row_of_tile, d),
        _mod_spec(layer, 4, row_of_tile, d),
        pl.BlockSpec((None, d, TN_PROJ), lambda i, j: (layer, 0, j)),
    ]
    args = [xs, g.reshape(1, d), mods, mods, w]
    rope_cols = q_cols = 0
    q_scale = 1.0
    if rope is not None:
        cos_t, sin_t, rope_cols, q_cols, q_scale, tiles_per_sample = rope
        assert q_cols <= rope_cols <= TN_PROJ
        in_specs += [pl.BlockSpec((tm, HEAD_DIM), lambda i, j: (i % tiles_per_sample, 0))] * 2
        args += [cos_t, sin_t]
    block_bytes = (2 * tm * d * 4 + tm * d * 2 + 2 * d * TN_PROJ * 2 + 2 * tm * TN_PROJ * 2
                   + tm * TN_PROJ * 4 + 4 * tm * HEAD_DIM * 4)
    return pl.pallas_call(
        functools.partial(_proj_kernel, rope_cols=rope_cols, q_cols=q_cols, q_scale=q_scale),
        out_shape=jax.ShapeDtypeStruct((rows, n), BF16),
        grid=(rows // tm, n // TN_PROJ),
        in_specs=in_specs,
        out_specs=pl.BlockSpec((tm, TN_PROJ), lambda i, j: (i, j)),
        scratch_shapes=[pltpu.VMEM((tm, d), BF16), pltpu.VMEM((tm, 1), F32)],
        compiler_params=_cparams(("parallel", "arbitrary"), block_bytes),
        name="in_proj_rope" if rope_cols else "in_proj",
    )(*args)


def _scores(q4, kcat):
    return lax.dot_general(q4, kcat, (((1,), (1,)), ((), ())), preferred_element_type=F32)


def _softmax_pv(s, vcat, sk, bias, scale):
    if scale is None:
        sk = sk * LOG2_E
        exp = jnp.exp2
    else:
        s = s * scale
        exp = jnp.exp
    if bias is not None:
        rb = bias.shape[0]
        s = jnp.concatenate([s[r0:r0 + rb] + bias for r0 in range(0, s.shape[0], rb)], axis=0)
    m = jnp.maximum(jnp.max(s, axis=-1, keepdims=True), sk)
    p = exp(s - m)
    denom = jnp.sum(p, axis=-1, keepdims=True) + exp(sk - m)
    o = jnp.dot(p.astype(BF16), vcat, preferred_element_type=F32)
    return o / denom


def _sink_rows(sink_ref, g, q_per_kv, rows):
    parts = [jnp.broadcast_to(sink_ref[g * q_per_kv + j:g * q_per_kv + j + 1, 0:1], (rows, 1))
             for j in range(q_per_kv)]
    return jnp.concatenate(parts, axis=0)


def _attn_lat_kernel(q_ref, kp_ref, kc_ref, kn_ref, vp_ref, vc_ref, vn_ref, kx_ref, vx_ref,
                     sink_ref, o_ref, *, nb, q_per_kv, scale):
    n = pl.program_id(1)
    blk = WINDOW_BLOCK
    dh = HEAD_DIM
    c_len = kx_ref.shape[0]
    nq = q_ref.shape[0] // blk
    keys = 3 * blk + c_len
    ri = lax.broadcasted_iota(jnp.int32, (blk, keys), 0)
    kj = lax.broadcasted_iota(jnp.int32, (blk, keys), 1)
    band = (kj >= 3 * blk) | ((kj >= ri) & (kj <= ri + 2 * blk))

    def window(qb, p_ref, c_ref, n_ref, x_ref, cols):
        parts = []
        for w in (qb - 1, qb, qb + 1):
            if w < 0:
                parts.append(p_ref[:, cols])
            elif w >= nq:
                parts.append(n_ref[:, cols])
            else:
                parts.append(c_ref[w * blk:(w + 1) * blk, cols])
        return jnp.concatenate(parts + [x_ref[:, cols]], axis=0)

    def scores(qb, g):
        q4 = jnp.concatenate([q_ref[qb * blk:(qb + 1) * blk, (g * q_per_kv + j) * dh:(g * q_per_kv + j + 1) * dh]
                              for j in range(q_per_kv)], axis=0)
        return _scores(q4, window(qb, kp_ref, kc_ref, kn_ref, kx_ref, slice(g * dh, (g + 1) * dh)))

    streams = [(qb, g) for qb in range(nq) for g in range(N_KV_HEADS)]
    s_next = scores(*streams[0])
    for i, (qb, g) in enumerate(streams):
        s_cur = s_next
        if i + 1 < len(streams):
            s_next = scores(*streams[i + 1])
        first = (n == 0) if qb == 0 else False
        last = (n == nb // nq - 1) if qb == nq - 1 else False
        lo = jnp.where(first, blk, 0)
        hi = jnp.where(last, 2 * blk, 3 * blk)
        bias = jnp.where(band & ((kj >= 3 * blk) | ((kj >= lo) & (kj < hi))), 0.0, NEG_BIG).astype(F32)
        vcat = window(qb, vp_ref, vc_ref, vn_ref, vx_ref, slice(g * dh, (g + 1) * dh))
        o = _softmax_pv(s_cur, vcat, _sink_rows(sink_ref, g, q_per_kv, blk), bias, scale)
        for j in range(q_per_kv):
            h = g * q_per_kv + j
            o_ref[qb * blk:(qb + 1) * blk, h * dh:(h + 1) * dh] = o[j * blk:(j + 1) * blk].astype(o_ref.dtype)


def _attn_ctx_kernel(q_ref, kx_ref, vx_ref, sink_ref, o_ref, *, q_per_kv, scale):
    dh = HEAD_DIM
    c_len = q_ref.shape[0]
    for g in range(N_KV_HEADS):
        cols = slice(g * dh, (g + 1) * dh)
        q4 = jnp.concatenate([q_ref[:, (g * q_per_kv + j) * dh:(g * q_per_kv + j + 1) * dh]
                              for j in range(q_per_kv)], axis=0)
        sk = _sink_rows(sink_ref, g, q_per_kv, c_len)
        o = _softmax_pv(_scores(q4, kx_ref[:, cols]), vx_ref[:, cols], sk, None, scale)
        for j in range(q_per_kv):
            h = g * q_per_kv + j
            o_ref[:, h * dh:(h + 1) * dh] = o[j * c_len:(j + 1) * c_len].astype(o_ref.dtype)


def _attention(p_lat, p_ctx, sink_b, bsz, s_len, c_len, attn_w, k_off, need_ctx):
    kv_w = N_KV_HEADS * HEAD_DIM
    n_q_heads = attn_w // HEAD_DIM
    q_per_kv = n_q_heads // N_KV_HEADS
    blk = WINDOW_BLOCK
    nb = s_len // blk
    k_col = k_off // kv_w
    v_col = k_col + 1
    scale = HEAD_DIM ** -0.5

    nq = ATTN_Q_BLOCKS
    steps = nb // nq
    assert nb % nq == 0

    def kv_specs(col):
        prev = pl.BlockSpec((blk, kv_w), lambda b, n: (b * nb + jnp.maximum(n * nq - 1, 0), col))
        own = pl.BlockSpec((nq * blk, kv_w), lambda b, n: (b * steps + n, col))
        nxt = pl.BlockSpec((blk, kv_w), lambda b, n: (b * nb + jnp.minimum((n + 1) * nq, nb - 1), col))
        return [prev, own, nxt]

    ya_lat = pl.pallas_call(
        functools.partial(_attn_lat_kernel, nb=nb, q_per_kv=q_per_kv, scale=None),
        out_shape=jax.ShapeDtypeStruct((bsz * s_len, attn_w), BF16),
        grid=(bsz, steps),
        in_specs=[pl.BlockSpec((nq * blk, attn_w), lambda b, n: (b * steps + n, 0))]
        + kv_specs(k_col) + kv_specs(v_col) + [
            pl.BlockSpec((c_len, kv_w), lambda b, n: (b, k_col)),
            pl.BlockSpec((c_len, kv_w), lambda b, n: (b, v_col)),
            pl.BlockSpec((n_q_heads, LANES), lambda b, n: (0, 0)),
        ],
        out_specs=pl.BlockSpec((nq * blk, attn_w), lambda b, n: (b * steps + n, 0)),
        compiler_params=_cparams(("parallel", "parallel"), 16 * 1024 * 1024),
        name="attn_latent",
    )(p_lat, p_lat, p_lat, p_lat, p_lat, p_lat, p_lat, p_ctx, p_ctx, sink_b)
    if not need_ctx:
        return ya_lat, None
    ya_ctx = pl.pallas_call(
        functools.partial(_attn_ctx_kernel, q_per_kv=q_per_kv, scale=scale),
        out_shape=jax.ShapeDtypeStruct((bsz * c_len, attn_w), BF16),
        grid=(bsz,),
        in_specs=[
            pl.BlockSpec((c_len, attn_w), lambda b: (b, 0)),
            pl.BlockSpec((c_len, kv_w), lambda b: (b, k_col)),
            pl.BlockSpec((c_len, kv_w), lambda b: (b, v_col)),
            pl.BlockSpec((n_q_heads, LANES), lambda b: (0, 0)),
        ],
        out_specs=pl.BlockSpec((c_len, attn_w), lambda b: (b, 0)),
        compiler_params=_cparams(("parallel",), 16 * 1024 * 1024),
        name="attn_context",
    )(p_ctx, p_ctx, p_ctx, sink_b)
    return ya_lat, ya_ctx


def _rglru_kernel(xp_ref, x_ref, xn_ref, cw_ref, cb_ref, w_ref, b_ref, lam_ref, h0_ref, *refs,
                  reverse, nk):
    if reverse:
        hf_ref, y_ref, hend_ref, xt_ref, hs_ref, h_ref = refs
    else:
        y_ref, hend_ref, xt_ref, hs_ref, h_ref = refs
    k = pl.program_id(1)
    ci = nk - 1 - k if reverse else k
    bsz, tc, width = x_ref.shape
    n_lb = width // LANES
    halo = xp_ref.shape[1]
    n_sub = tc // SCAN_SUB
    rows_sub = SCAN_SUB * bsz

    @pl.when(k == 0)
    def _():
        h_ref[...] = h0_ref[...]

    for c in range(n_lb):
        lanes = slice(c * LANES, (c + 1) * LANES)
        for b in range(bsz):
            xt_ref[c, pl.ds(b, halo, stride=bsz), :] = xp_ref[b, :, lanes].astype(F32)
            xt_ref[c, pl.ds(halo * bsz + b, tc, stride=bsz), :] = x_ref[b, :, lanes].astype(F32)
            xt_ref[c, pl.ds((halo + tc) * bsz + b, halo, stride=bsz), :] = xn_ref[b, :, lanes].astype(F32)

    @pl.when(ci == 0)
    def _():
        xt_ref[:, 0:halo * bsz, :] = jnp.zeros((n_lb, halo * bsz, LANES), F32)

    @pl.when(ci == nk - 1)
    def _():
        xt_ref[:, (halo + tc) * bsz:(2 * halo + tc) * bsz, :] = jnp.zeros((n_lb, halo * bsz, LANES), F32)

    def sub_body(i, hs):
        si = (n_sub - 1 - i) if reverse else i
        t0 = si * SCAN_SUB
        new_h = []
        for c in range(n_lb):
            lanes = slice(c * LANES, (c + 1) * LANES)
            cw = cw_ref[:, lanes]
            u = cb_ref[:, lanes]
            for kk in range(cw.shape[0]):
                r0 = pl.multiple_of((t0 + halo - CONV_LEFT + kk) * bsz, bsz)
                u = u + cw[kk:kk + 1, :] * xt_ref[c, pl.ds(r0, rows_sub), :]
            g = jnp.dot(u.astype(BF16), w_ref[c], preferred_element_type=F32)
            gr = jax.nn.sigmoid(g[:, :LANES] + b_ref[0:1, lanes])
            gi = jax.nn.sigmoid(g[:, LANES:] + b_ref[1:2, lanes])
            neg = -lam_ref[:, lanes]
            softplus = jnp.maximum(neg, 0.0) + jnp.log1p(jnp.exp(-jnp.abs(neg)))
            log_a = gr * ((-RGLRU_C) * softplus)
            a = jnp.exp(log_a)
            x1 = -jnp.tanh(log_a) * (a * a + 1.0)
            m = jnp.where(x1 > 0.0, x1 * lax.rsqrt(x1), 0.0)
            bx = m * (gi * u)
            h = hs[c]
            outs = [None] * SCAN_SUB
            for t in (range(SCAN_SUB - 1, -1, -1) if reverse else range(SCAN_SUB)):
                h = a[t * bsz:(t + 1) * bsz] * h + bx[t * bsz:(t + 1) * bsz]
                outs[t] = h
            hs_ref[c, pl.ds(pl.multiple_of(t0 * bsz, rows_sub), rows_sub), :] = jnp.concatenate(outs, axis=0)
            new_h.append(h)
        return tuple(new_h)

    h_init = tuple(h_ref[:, c * LANES:(c + 1) * LANES] for c in range(n_lb))
    h_fin = lax.fori_loop(0, n_sub, sub_body, h_init)
    for c in range(n_lb):
        lanes = slice(c * LANES, (c + 1) * LANES)
        h_ref[:, lanes] = h_fin[c]
        for b in range(bsz):
            yb = hs_ref[c, pl.ds(b, tc, stride=bsz), :]
            if reverse:
                yb = yb + hf_ref[b, :, lanes]
            y_ref[b, :, lanes] = yb.astype(y_ref.dtype)

    @pl.when(k == nk - 1)
    def _():
        hend_ref[...] = h_ref[...]


def _rglru_dir(p3, col0, layer, conv_w, conv_b, w_gates, b_gates, lam, h0, hf, reverse):
    direction = 1 if reverse else 0
    bsz, seq, _ = p3.shape
    rnn_w = conv_w.shape[-1]
    width = SCAN_WIDTH
    tc = min(SCAN_CHUNK, seq)
    nk = seq // tc
    n_lb = width // LANES
    cblk0 = col0 // width
    halo = 2 * SUBLANES
    hpt = tc // halo
    n_halo_blocks = seq // halo

    def tmap(k):
        return nk - 1 - k if reverse else k

    in_specs = [
        pl.BlockSpec((bsz, halo, width), lambda j, k: (0, jnp.maximum(tmap(k) * hpt - 1, 0), cblk0 + j)),
        pl.BlockSpec((bsz, tc, width), lambda j, k: (0, tmap(k), cblk0 + j)),
        pl.BlockSpec((bsz, halo, width),
                     lambda j, k: (0, jnp.minimum((tmap(k) + 1) * hpt, n_halo_blocks - 1), cblk0 + j)),
        pl.BlockSpec((conv_w.shape[0], width), lambda j, k: (0, j)),
        pl.BlockSpec((1, width), lambda j, k: (0, j)),
        pl.BlockSpec((None, None, n_lb, LANES, 2 * LANES), lambda j, k: (layer, direction, j, 0, 0)),
        pl.BlockSpec((2, width), lambda j, k: (0, j)),
        pl.BlockSpec((1, width), lambda j, k: (0, j)),
        pl.BlockSpec((bsz, width), lambda j, k: (0, j)),
    ]
    args = [p3, p3, p3, conv_w, conv_b.reshape(1, rnn_w), w_gates, b_gates, lam.reshape(1, rnn_w), h0]
    if reverse:
        in_specs.append(pl.BlockSpec((bsz, tc, width), lambda j, k: (0, tmap(k), j)))
        args.append(hf)
    out_dtype = BF16 if reverse else F32
    block_bytes = (2 * bsz * tc * width * 2 + 2 * bsz * tc * width * 4 * 2
                   + (tc + 2 * halo) * bsz * width * 4 + tc * bsz * width * 4)
    return pl.pallas_call(
        functools.partial(_rglru_kernel, reverse=reverse, nk=nk),
        out_shape=(jax.ShapeDtypeStruct((bsz, seq, rnn_w), out_dtype),
                   jax.ShapeDtypeStruct((bsz, rnn_w), F32)),
        grid=(rnn_w // width, nk),
        in_specs=in_specs,
        out_specs=(pl.BlockSpec((bsz, tc, width), lambda j, k: (0, tmap(k), j)),
                   pl.BlockSpec((bsz, width), lambda j, k: (0, j))),
        scratch_shapes=[pltpu.VMEM((n_lb, (tc + 2 * halo) * bsz, LANES), F32),
                        pltpu.VMEM((n_lb, tc * bsz, LANES), F32),
                        pltpu.VMEM((bsz, width), F32)],
        compiler_params=_cparams(("parallel", "arbitrary"), block_bytes),
        name="rglru_bwd" if reverse else "rglru_fwd",
    )(*args)


def _rglru(p3, col0, layer, conv_w, conv_b, w_gates, b_gates, lam, h0):
    hf, end_f = _rglru_dir(p3, col0, layer, conv_w, conv_b, w_gates, b_gates[0], lam[0], h0[0], None, False)
    y, end_b = _rglru_dir(p3, col0, layer, conv_w, conv_b, w_gates, b_gates[1], lam[1], h0[1], hf, True)
    return y, (end_f, end_b)


def _gelu_tanh(x):
    return 0.5 * x * (1.0 + jnp.tanh(0.7978845608028654 * (x + 0.044715 * (x * x * x))))


def _merge_kernel(x_ref, ya_ref, r_ref, gr0_ref, gr1_ref, ga0_ref, ga1_ref, gb0_ref, gb1_ref, gt_ref,
                  woa_ref, wor_ref, wout_ref, o_ref, yr_ref, acc_ref):
    c = pl.program_id(1)

    def wide(lo_ref, hi_ref):
        return jnp.concatenate([lo_ref[...], hi_ref[...]], axis=1).astype(F32)

    @pl.when(c == 0)
    def _():
        yr_ref[...] = (r_ref[...].astype(F32) * _gelu_tanh(wide(gr0_ref, gr1_ref))).astype(yr_ref.dtype)
        acc_ref[...] = jnp.zeros_like(acc_ref)

    pa = jnp.dot(ya_ref[...], woa_ref[...], preferred_element_type=F32)
    pr = jnp.dot(yr_ref[...], wor_ref[...], preferred_element_type=F32)
    t = jax.nn.sigmoid(wide(ga0_ref, ga1_ref)) * pa + jax.nn.sigmoid(wide(gb0_ref, gb1_ref)) * pr
    acc_ref[...] += jnp.dot(t.astype(BF16), wout_ref[...], preferred_element_type=F32)

    @pl.when(c == pl.num_programs(1) - 1)
    def _():
        o_ref[...] = x_ref[...] + gt_ref[...] * acc_ref[...]


def _merge(xs, ya, r, proj, gr_off, ga_off, gb_off, mods, layer, woa, wor, wout, row_of_tile):
    rows, d = xs.shape
    attn_w = ya.shape[1]
    rnn_w = r.shape[1]
    tm, tc = TM_MERGE, TC_MERGE
    half = tc // 2
    assert rnn_w == tc and gr_off % half == 0 and ga_off % half == 0 and gb_off % half == 0

    def half_spec(off, k):
        return pl.BlockSpec((tm, half), lambda i, c: (i, off // half + 2 * c + k))

    def gr_spec(k):
        return pl.BlockSpec((tm, half), lambda i, c: (i, gr_off // half + k))

    block_bytes = (4 * tm * d * 4 + 2 * tm * (attn_w + 2 * rnn_w + 2 * tc) * 2
                   + 2 * (attn_w + rnn_w + d) * tc * 2 + tm * rnn_w * 2 + tm * d * 4 + 3 * tm * tc * 4)
    return pl.pallas_call(
        _merge_kernel,
        out_shape=jax.ShapeDtypeStruct((rows, d), F32),
        grid=(rows // tm, d // tc),
        in_specs=[
            pl.BlockSpec((tm, d), lambda i, c: (i, 0)),
            pl.BlockSpec((tm, attn_w), lambda i, c: (i, 0)),
            pl.BlockSpec((tm, rnn_w), lambda i, c: (i, 0)),
            gr_spec(0), gr_spec(1),
            half_spec(ga_off, 0), half_spec(ga_off, 1),
            half_spec(gb_off, 0), half_spec(gb_off, 1),
            _mod_spec(layer, 5, row_of_tile, d),
            pl.BlockSpec((None, attn_w, tc), lambda i, c: (layer, 0, c)),
            pl.BlockSpec((None, rnn_w, tc), lambda i, c: (layer, 0, c)),
            pl.BlockSpec((None, tc, d), lambda i, c: (layer, c, 0)),
        ],
        out_specs=pl.BlockSpec((tm, d), lambda i, c: (i, 0)),
        scratch_shapes=[pltpu.VMEM((tm, rnn_w), BF16), pltpu.VMEM((tm, d), F32)],
        input_output_aliases={0: 0},
        compiler_params=_cparams(("parallel", "arbitrary"), block_bytes),
        name="merge",
    )(xs, ya, r, proj, proj, proj, proj, proj, proj, mods, woa, wor, wout)


def _up_cast_kernel(w_ref, g_ref, u_ref, *, d_ff):
    pad = g_ref.shape[1] - d_ff
    g_ref[:, :d_ff] = w_ref[:, :d_ff].astype(g_ref.dtype)
    u_ref[:, :d_ff] = w_ref[:, d_ff:].astype(u_ref.dtype)
    if pad:
        g_ref[:, d_ff:] = jnp.zeros((g_ref.shape[0], pad), g_ref.dtype)
        u_ref[:, d_ff:] = jnp.zeros((u_ref.shape[0], pad), u_ref.dtype)


def _ffn_up_weights(w_up, d_ff, ff_pad):
    depth, halves, d, n = w_up.shape
    assert n == 2 * d_ff and d_ff % LANES == 0 and d % CAST_ROWS == 0
    out = jax.ShapeDtypeStruct((depth * halves, d, ff_pad), BF16)
    wg, wu = pl.pallas_call(
        functools.partial(_up_cast_kernel, d_ff=d_ff),
        out_shape=(out, out),
        grid=(depth * halves, d // CAST_ROWS),
        in_specs=[pl.BlockSpec((None, CAST_ROWS, n), lambda l, r: (l, r, 0))],
        out_specs=(pl.BlockSpec((None, CAST_ROWS, ff_pad), lambda l, r: (l, r, 0)),
                   pl.BlockSpec((None, CAST_ROWS, ff_pad), lambda l, r: (l, r, 0))),
        compiler_params=_cparams(("parallel", "parallel"), 2 * CAST_ROWS * (n * 4 + 2 * ff_pad * 2)),
        name="ffn_up_cast",
    )(w_up.reshape(depth * halves, d, n))
    return wg.reshape(depth, halves, d, ff_pad), wu.reshape(depth, halves, d, ff_pad)


def _down_cast_kernel(w_ref, o_ref, *, tail_rows):
    r = pl.program_id(1)
    last = pl.num_programs(1) - 1

    @pl.when(r != last)
    def _():
        o_ref[...] = w_ref[...].astype(o_ref.dtype)

    @pl.when(r == last)
    def _():
        o_ref[:tail_rows, :] = w_ref[:tail_rows, :].astype(o_ref.dtype)
        if tail_rows < o_ref.shape[0]:
            o_ref[tail_rows:, :] = jnp.zeros((o_ref.shape[0] - tail_rows, o_ref.shape[1]), o_ref.dtype)


def _ffn_down_weights(w_down, ff_pad):
    depth, halves, d_ff, d = w_down.shape
    n_blocks = ff_pad // TF
    tail_rows = d_ff - (n_blocks - 1) * TF
    assert 0 < tail_rows <= TF and tail_rows % (2 * SUBLANES) == 0
    wd = pl.pallas_call(
        functools.partial(_down_cast_kernel, tail_rows=tail_rows),
        out_shape=jax.ShapeDtypeStruct((depth * halves, ff_pad, d), BF16),
        grid=(depth * halves, n_blocks),
        in_specs=[pl.BlockSpec((None, TF, d), lambda l, r: (l, r, 0))],
        out_specs=pl.BlockSpec((None, TF, d), lambda l, r: (l, r, 0)),
        compiler_params=_cparams(("parallel", "parallel"), 2 * TF * d * (4 + 2)),
        name="ffn_down_cast",
    )(w_down.reshape(depth * halves, d_ff, d))
    return wd.reshape(depth, halves, ff_pad, d)


def _rope_tables(s_len):
    pos = jnp.arange(s_len, dtype=jnp.int32)
    row = (pos // GRID_W).astype(F32)
    col = (pos % GRID_W).astype(F32)
    inv_freq = ROPE_BASE ** (-jnp.arange(ROPE_FREQS, dtype=F32) / ROPE_FREQS)
    ang_r = row[:, None] * inv_freq
    ang_c = col[:, None] * inv_freq
    cos_t = jnp.concatenate([jnp.cos(ang_r), jnp.cos(ang_r), jnp.cos(ang_c), jnp.cos(ang_c)], axis=1)
    sin_t = jnp.concatenate([-jnp.sin(ang_r), jnp.sin(ang_r), -jnp.sin(ang_c), jnp.sin(ang_c)], axis=1)
    return cos_t, sin_t


def kernel(x, c, ctx, c_ctx, w_ada, b_ada, norm_g, final_g, w_ffn_up, w_ffn_down, w_in, attn_sink,
           conv_w, conv_b, rg_w, rg_b, rg_lambda, w_o_attn, w_o_rnn, w_out):
    bsz, s_len, d = x.shape
    c_len = ctx.shape[1]
    depth = w_ada.shape[0]
    d_ff = w_ffn_down.shape[2]
    attn_w = w_o_attn.shape[1]
    rnn_w = w_o_rnn.shape[1]
    kv_w = N_KV_HEADS * HEAD_DIM
    ff_pad = -(-d_ff // TF) * TF
    n_lat, n_ctx = bsz * s_len, bsz * c_len
    assert s_len % TM_FFN == 0 and n_ctx % TM_FFN == 0 and s_len % TM_MERGE == 0 and n_ctx % TM_MERGE == 0
    assert bsz == SUBLANES and bsz + 1 <= MOD_ROWS and rnn_w % SCAN_WIDTH == 0
    assert s_len % SCAN_CHUNK == 0 and c_len % SCAN_SUB == 0 and c_len <= SCAN_CHUNK

    off_k = attn_w
    off_xr = off_k + 2 * kv_w
    off_gr = off_xr + rnn_w
    off_ga = off_gr + rnn_w
    off_gb = off_ga + d
    assert off_xr == TN_PROJ and off_k % kv_w == 0 and off_xr % SCAN_WIDTH == 0

    def lat_row(tm):
        return lambda i: i // (s_len // tm)

    def ctx_row(i):
        return bsz

    wg_all, wu_all = _ffn_up_weights(w_ffn_up, d_ff, ff_pad)
    wd_all = _ffn_down_weights(w_ffn_down, ff_pad)
    w_in_all = w_in.astype(BF16)
    woa_all, wor_all, wout_all = w_o_attn.astype(BF16), w_o_rnn.astype(BF16), w_out.astype(BF16)
    w_gates_all = jnp.concatenate([rg_w[:, :, 0], rg_w[:, :, 1]], axis=-1).astype(BF16)

    c_all = jnp.zeros((MOD_ROWS, d), F32).at[:bsz].set(c).at[bsz].set(c_ctx)
    mods = _adaln(c_all, w_ada, b_ada).reshape(depth, MOD_ROWS, 1, N_MOD * d)
    cos_t, sin_t = _rope_tables(s_len)
    rope = (cos_t, sin_t, attn_w + kv_w, attn_w, HEAD_DIM ** -0.5 * LOG2_E, s_len // TM_FFN)
    xl = x.reshape(n_lat, d)
    xc = ctx.reshape(n_ctx, d)
    h_zero = (jnp.zeros((bsz, rnn_w), F32), jnp.zeros((bsz, rnn_w), F32))

    for l in range(depth):
        need_ctx = l < depth - 1
        first = l == 0
        xl = _ffn(xl, mods, l, 0, norm_g[l, 0], wg_all, wu_all, wd_all, lat_row(TM_FFN), in_place=not first)
        xc = _ffn(xc, mods, l, 0, norm_g[l, 0], wg_all, wu_all, wd_all, ctx_row, in_place=not first)

        p_lat = _proj(xl, mods, l, norm_g[l, 1], w_in_all, lat_row(TM_FFN), rope=rope)
        p_ctx = _proj(xc, mods, l, norm_g[l, 1], w_in_all, ctx_row)

        sink_b = jnp.broadcast_to(attn_sink[l][:, None], (attn_sink.shape[1], LANES))
        ya_lat, ya_ctx = _attention(p_lat, p_ctx, sink_b, bsz, s_len, c_len, attn_w, off_k, need_ctx)

        r_ctx, h_end = _rglru(p_ctx.reshape(bsz, c_len, -1), off_xr, l, conv_w[l], conv_b[l], w_gates_all,
                              rg_b[l], rg_lambda[l], h_zero)
        r_lat, _ = _rglru(p_lat.reshape(bsz, s_len, -1), off_xr, l, conv_w[l], conv_b[l], w_gates_all,
                          rg_b[l], rg_lambda[l], h_end)

        xl = _merge(xl, ya_lat, r_lat.reshape(n_lat, rnn_w), p_lat, off_gr, off_ga, off_gb, mods, l,
                    woa_all, wor_all, wout_all, lat_row(TM_MERGE))
        if need_ctx:
            xc = _merge(xc, ya_ctx, r_ctx.reshape(n_ctx, rnn_w), p_ctx, off_gr, off_ga, off_gb, mods, l,
                        woa_all, wor_all, wout_all, ctx_row)
            xc = _ffn(xc, mods, l, 1, norm_g[l, 2], wg_all, wu_all, wd_all, ctx_row)
        xl = _ffn(xl, mods, l, 1, norm_g[l, 2], wg_all, wu_all, wd_all, lat_row(TM_FFN),
                  final_g=None if need_ctx else final_g)

    return xl.reshape(bsz, s_len, d)
```

```python
import functools

import jax
import jax.numpy as jnp
from jax import lax
from jax.experimental import pallas as pl
from jax.experimental.pallas import tpu as pltpu

F32 = jnp.float32
BF16 = jnp.bfloat16

LANES = 128
SUBLANES = 8
VMEM_PHYSICAL_BYTES = 64 * 1024 * 1024
VMEM_HEADROOM_BYTES = 4 * 1024 * 1024

HEAD_DIM = 128
N_KV_HEADS = 2
WINDOW_BLOCK = 128
GRID_W = 64
ROPE_BASE = 10000.0
ROPE_FREQS = HEAD_DIM // 4
CONV_LEFT = 2
RGLRU_C = 8.0
N_MOD = 9
EPS = 1e-6
MOD_ROWS = 16
NEG_BIG = -1e30
LOG2_E = 1.4426950408889634

TM_FFN = 1024
TF = 512
TN_PROJ = 1536
TM_MERGE = 512
TC_MERGE = 1024
TN_ADA = 1024
NORM_ROWS = 128
NORM_UNROLL = 4
SCAN_CHUNK = 256
SCAN_SUB = 64
SCAN_WIDTH = 512
ATTN_Q_BLOCKS = 8
CAST_ROWS = 256


def _cparams(sem, block_bytes):
    limit = min(int(block_bytes) + VMEM_HEADROOM_BYTES, VMEM_PHYSICAL_BYTES - VMEM_HEADROOM_BYTES // 2)
    return pltpu.CompilerParams(dimension_semantics=sem, vmem_limit_bytes=limit)


def _adaln_kernel(c_ref, w_ref, b_ref, o_ref):
    c = c_ref[...]
    sc = c * jax.nn.sigmoid(c)
    o_ref[...] = jnp.dot(sc, w_ref[...], preferred_element_type=F32,
                         precision=lax.Precision.HIGHEST) + b_ref[...]


def _adaln(c_all, w_ada, b_ada):
    depth, d, n = w_ada.shape
    return pl.pallas_call(
        _adaln_kernel,
        out_shape=jax.ShapeDtypeStruct((depth, MOD_ROWS, n), F32),
        grid=(depth, n // TN_ADA),
        in_specs=[
            pl.BlockSpec((MOD_ROWS, d), lambda l, j: (0, 0)),
            pl.BlockSpec((None, d, TN_ADA), lambda l, j: (l, 0, j)),
            pl.BlockSpec((None, 1, TN_ADA), lambda l, j: (l, 0, j)),
        ],
        out_specs=pl.BlockSpec((None, MOD_ROWS, TN_ADA), lambda l, j: (l, 0, j)),
        compiler_params=_cparams(("parallel", "parallel"), 2 * d * TN_ADA * 4 + 4 * MOD_ROWS * (d + TN_ADA) * 4),
        name="adaln",
    )(c_all, w_ada, b_ada.reshape(depth, 1, n))


def _rms_scale(x):
    return x * lax.rsqrt(jnp.mean(x * x, axis=-1, keepdims=True) + EPS)


def _norm_mod_store(xn_ref, rs_ref, x_ref, g_ref, sh_ref, sc_ref, zero_ref=None):
    n_steps = x_ref.shape[0] // NORM_ROWS
    inv_d = 1.0 / x_ref.shape[1]
    gs = g_ref[...] * (1.0 + sc_ref[...])
    sh = sh_ref[...]

    def stats(i, carry):
        r0 = pl.multiple_of(i * NORM_ROWS, NORM_ROWS)
        x = x_ref[pl.ds(r0, NORM_ROWS), :]
        rs_ref[pl.ds(r0, NORM_ROWS), :] = lax.rsqrt(jnp.sum(x * x, axis=-1, keepdims=True) * inv_d + EPS)
        return carry

    lax.fori_loop(0, n_steps, stats, 0, unroll=NORM_UNROLL)

    def scale(i, carry):
        r0 = pl.multiple_of(i * NORM_ROWS, NORM_ROWS)
        y = (x_ref[pl.ds(r0, NORM_ROWS), :] * rs_ref[pl.ds(r0, NORM_ROWS), :]) * gs + sh
        xn_ref[pl.ds(r0, NORM_ROWS), :] = y.astype(xn_ref.dtype)
        if zero_ref is not None:
            zero_ref[pl.ds(r0, NORM_ROWS), :] = jnp.zeros((NORM_ROWS, zero_ref.shape[1]), zero_ref.dtype)
        return carry

    lax.fori_loop(0, n_steps, scale, 0)


def _mod_spec(layer, k, row_of_tile, d):
    return pl.BlockSpec((None, None, 1, d), lambda i, j: (layer, row_of_tile(i), 0, k))


def _ffn_kernel(x_ref, g_ref, sh_ref, sc_ref, gt_ref, wg_ref, wu_ref, wd_ref, *refs, final_norm):
    if final_norm:
        fg_ref, o_ref, xn_ref, rs_ref = refs
    else:
        o_ref, xn_ref, rs_ref = refs
    c = pl.program_id(1)

    @pl.when(c == 0)
    def _():
        _norm_mod_store(xn_ref, rs_ref, x_ref, g_ref, sh_ref, sc_ref, zero_ref=o_ref)

    xn = xn_ref[...]
    h = jnp.dot(xn, wg_ref[...], preferred_element_type=F32)
    u = jnp.dot(xn, wu_ref[...], preferred_element_type=F32)
    a = ((h * jax.nn.sigmoid(h)) * u).astype(BF16)
    o_ref[...] += jnp.dot(a, wd_ref[...], preferred_element_type=F32)

    @pl.when(c == pl.num_programs(1) - 1)
    def _():
        gate = 0.5 * gt_ref[...]

        def body(i, carry):
            r0 = pl.multiple_of(i * NORM_ROWS, NORM_ROWS)
            y = x_ref[pl.ds(r0, NORM_ROWS), :] + gate * o_ref[pl.ds(r0, NORM_ROWS), :]
            if final_norm:
                y = _rms_scale(y) * fg_ref[...]
            o_ref[pl.ds(r0, NORM_ROWS), :] = y
            return carry

        lax.fori_loop(0, x_ref.shape[0] // NORM_ROWS, body, 0)


def _ffn(xs, mods, layer, half, g, wg, wu, wd, row_of_tile, final_g=None, in_place=True):
    rows, d = xs.shape
    ff = wg.shape[-1]
    tm = TM_FFN
    mod_k0 = 6 * half
    final_norm = final_g is not None
    in_specs = [
        pl.BlockSpec((tm, d), lambda i, c: (i, 0)),
        pl.BlockSpec((1, d), lambda i, c: (0, 0)),
        _mod_spec(layer, mod_k0, row_of_tile, d),
        _mod_spec(layer, mod_k0 + 1, row_of_tile, d),
        _mod_spec(layer, mod_k0 + 2, row_of_tile, d),
        pl.BlockSpec((None, None, d, TF), lambda i, c: (layer, half, 0, c)),
        pl.BlockSpec((None, None, d, TF), lambda i, c: (layer, half, 0, c)),
        pl.BlockSpec((None, None, TF, d), lambda i, c: (layer, half, c, 0)),
    ]
    args = [xs, g.reshape(1, d), mods, mods, mods, wg, wu, wd]
    if final_norm:
        in_specs.append(pl.BlockSpec((1, d), lambda i, c: (0, 0)))
        args.append(final_g.reshape(1, d))
    block_bytes = (2 * tm * d * 4 + 2 * tm * d * 4 + 2 * 3 * d * TF * 2 + tm * d * 2
                   + 2 * tm * TF * 4 + tm * TF * 2)
    return pl.pallas_call(
        functools.partial(_ffn_kernel, final_norm=final_norm),
        out_shape=jax.ShapeDtypeStruct((rows, d), F32),
        grid=(rows // tm, ff // TF),
        in_specs=in_specs,
        out_specs=pl.BlockSpec((tm, d), lambda i, c: (i, 0)),
        scratch_shapes=[pltpu.VMEM((tm, d), BF16), pltpu.VMEM((tm, 1), F32)],
        input_output_aliases={0: 0} if (in_place and not final_norm) else {},
        compiler_params=_cparams(("parallel", "arbitrary"), block_bytes),
        name="ffn_final" if final_norm else "ffn",
    )(*args)


def _rope(t, cs, sn):
    lane = lax.broadcasted_iota(jnp.int32, t.shape, 1)
    first_half = (lane & (2 * ROPE_FREQS - 1)) < ROPE_FREQS
    partner = jnp.where(first_half,
                        pltpu.roll(t, HEAD_DIM - ROPE_FREQS, axis=1),
                        pltpu.roll(t, ROPE_FREQS, axis=1))
    return t * cs + partner * sn


def _proj_kernel(x_ref, g_ref, sh_ref, sc_ref, w_ref, *refs, rope_cols, q_cols, q_scale):
    if rope_cols:
        cos_ref, sin_ref, o_ref, xn_ref, rs_ref = refs
    else:
        o_ref, xn_ref, rs_ref = refs
    j = pl.program_id(1)

    @pl.when(j == 0)
    def _():
        _norm_mod_store(xn_ref, rs_ref, x_ref, g_ref, sh_ref, sc_ref)

    def plain():
        o_ref[...] = jnp.dot(xn_ref[...], w_ref[...], preferred_element_type=F32).astype(o_ref.dtype)

    if not rope_cols:
        plain()
        return

    @pl.when(j == 0)
    def _():
        xn = xn_ref[...]
        cs = cos_ref[...]
        sn = sin_ref[...]
        pair = 2 * HEAD_DIM
        for c0 in range(0, o_ref.shape[1], pair):
            res = jnp.dot(xn, w_ref[:, c0:c0 + pair], preferred_element_type=F32)
            for h0 in range(0, pair, HEAD_DIM):
                t = res[:, h0:h0 + HEAD_DIM]
                if c0 + h0 < rope_cols:
                    t = _rope(t, cs, sn)
                if c0 + h0 < q_cols:
                    t = t * q_scale
                o_ref[:, c0 + h0:c0 + h0 + HEAD_DIM] = t.astype(o_ref.dtype)

    pl.when(j != 0)(plain)


def _proj(xs, mods, layer, g, w, row_of_tile, rope=None):
    rows, d = xs.shape
    n = w.shape[-1]
    tm = TM_FFN
    in_specs = [
        pl.BlockSpec((tm, d), lambda i, j: (i, 0)),
        pl.BlockSpec((1, d), lambda i, j: (0, 0)),
        _mod_spec(layer, 3, row_of_tile, d),
        _mod_spec(layer, 4, row_of_tile, d),
        pl.BlockSpec((None, d, TN_PROJ), lambda i, j: (layer, 0, j)),
    ]
    args = [xs, g.reshape(1, d), mods, mods, w]
    rope_cols = q_cols = 0
    q_scale = 1.0
    if rope is not None:
        cos_t, sin_t, rope_cols, q_cols, q_scale, tiles_per_sample = rope
        assert q_cols <= rope_cols <= TN_PROJ
        in_specs += [pl.BlockSpec((tm, HEAD_DIM), lambda i, j: (i % tiles_per_sample, 0))] * 2
        args += [cos_t, sin_t]
    block_bytes = (2 * tm * d * 4 + tm * d * 2 + 2 * d * TN_PROJ * 2 + 2 * tm * TN_PROJ * 2
                   + tm * TN_PROJ * 4 + 4 * tm * HEAD_DIM * 4)
    return pl.pallas_call(
        functools.partial(_proj_kernel, rope_cols=rope_cols, q_cols=q_cols, q_scale=q_scale),
        out_shape=jax.ShapeDtypeStruct((rows, n), BF16),
        grid=(rows // tm, n // TN_PROJ),
        in_specs=in_specs,
        out_specs=pl.BlockSpec((tm, TN_PROJ), lambda i, j: (i, j)),
        scratch_shapes=[pltpu.VMEM((tm, d), BF16), pltpu.VMEM((tm, 1), F32)],
        compiler_params=_cparams(("parallel", "arbitrary"), block_bytes),
        name="in_proj_rope" if rope_cols else "in_proj",
    )(*args)


def _scores(q4, kcat):
    return lax.dot_general(q4, kcat, (((1,), (1,)), ((), ())), preferred_element_type=F32)


def _softmax_pv(s, vcat, sk, bias, scale):
    if scale is None:
        sk = sk * LOG2_E
        exp = jnp.exp2
    else:
        s = s * scale
        exp = jnp.exp
    if bias is not None:
        rb = bias.shape[0]
        s = jnp.concatenate([s[r0:r0 + rb] + bias for r0 in range(0, s.shape[0], rb)], axis=0)
    m = jnp.maximum(jnp.max(s, axis=-1, keepdims=True), sk)
    p = exp(s - m)
    denom = jnp.sum(p, axis=-1, keepdims=True) + exp(sk - m)
    o = jnp.dot(p.astype(BF16), vcat, preferred_element_type=F32)
    return o / denom


def _sink_rows(sink_ref, g, q_per_kv, rows):
    parts = [jnp.broadcast_to(sink_ref[g * q_per_kv + j:g * q_per_kv + j + 1, 0:1], (rows, 1))
             for j in range(q_per_kv)]
    return jnp.concatenate(parts, axis=0)


def _attn_lat_kernel(q_ref, kp_ref, kc_ref, kn_ref, vp_ref, vc_ref, vn_ref, kx_ref, vx_ref,
                     sink_ref, o_ref, *, nb, q_per_kv, scale):
    n = pl.program_id(1)
    blk = WINDOW_BLOCK
    dh = HEAD_DIM
    c_len = kx_ref.shape[0]
    nq = q_ref.shape[0] // blk
    keys = 3 * blk + c_len
    ri = lax.broadcasted_iota(jnp.int32, (blk, keys), 0)
    kj = lax.broadcasted_iota(jnp.int32, (blk, keys), 1)
    band = (kj >= 3 * blk) | ((kj >= ri) & (kj <= ri + 2 * blk))

    def window(qb, p_ref, c_ref, n_ref, x_ref, cols):
        parts = []
        for w in (qb - 1, qb, qb + 1):
            if w < 0:
                parts.append(p_ref[:, cols])
            elif w >= nq:
                parts.append(n_ref[:, cols])
            else:
                parts.append(c_ref[w * blk:(w + 1) * blk, cols])
        return jnp.concatenate(parts + [x_ref[:, cols]], axis=0)

    def scores(qb, g):
        q4 = jnp.concatenate([q_ref[qb * blk:(qb + 1) * blk, (g * q_per_kv + j) * dh:(g * q_per_kv + j + 1) * dh]
                              for j in range(q_per_kv)], axis=0)
        return _scores(q4, window(qb, kp_ref, kc_ref, kn_ref, kx_ref, slice(g * dh, (g + 1) * dh)))

    streams = [(qb, g) for qb in range(nq) for g in range(N_KV_HEADS)]
    s_next = scores(*streams[0])
    for i, (qb, g) in enumerate(streams):
        s_cur = s_next
        if i + 1 < len(streams):
            s_next = scores(*streams[i + 1])
        first = (n == 0) if qb == 0 else False
        last = (n == nb // nq - 1) if qb == nq - 1 else False
        lo = jnp.where(first, blk, 0)
        hi = jnp.where(last, 2 * blk, 3 * blk)
        bias = jnp.where(band & ((kj >= 3 * blk) | ((kj >= lo) & (kj < hi))), 0.0, NEG_BIG).astype(F32)
        vcat = window(qb, vp_ref, vc_ref, vn_ref, vx_ref, slice(g * dh, (g + 1) * dh))
        o = _softmax_pv(s_cur, vcat, _sink_rows(sink_ref, g, q_per_kv, blk), bias, scale)
        for j in range(q_per_kv):
            h = g * q_per_kv + j
            o_ref[qb * blk:(qb + 1) * blk, h * dh:(h + 1) * dh] = o[j * blk:(j + 1) * blk].astype(o_ref.dtype)


def _attn_ctx_kernel(q_ref, kx_ref, vx_ref, sink_ref, o_ref, *, q_per_kv, scale):
    dh = HEAD_DIM
    c_len = q_ref.shape[0]
    for g in range(N_KV_HEADS):
        cols = slice(g * dh, (g + 1) * dh)
        q4 = jnp.concatenate([q_ref[:, (g * q_per_kv + j) * dh:(g * q_per_kv + j + 1) * dh]
                              for j in range(q_per_kv)], axis=0)
        sk = _sink_rows(sink_ref, g, q_per_kv, c_len)
        o = _softmax_pv(_scores(q4, kx_ref[:, cols]), vx_ref[:, cols], sk, None, scale)
        for j in range(q_per_kv):
            h = g * q_per_kv + j
            o_ref[:, h * dh:(h + 1) * dh] = o[j * c_len:(j + 1) * c_len].astype(o_ref.dtype)


def _attention(p_lat, p_ctx, sink_b, bsz, s_len, c_len, attn_w, k_off, need_ctx):
    kv_w = N_KV_HEADS * HEAD_DIM
    n_q_heads = attn_w // HEAD_DIM
    q_per_kv = n_q_heads // N_KV_HEADS
    blk = WINDOW_BLOCK
    nb = s_len // blk
    k_col = k_off // kv_w
    v_col = k_col + 1
    scale = HEAD_DIM ** -0.5

    nq = ATTN_Q_BLOCKS
    steps = nb // nq
    assert nb % nq == 0

    def kv_specs(col):
        prev = pl.BlockSpec((blk, kv_w), lambda b, n: (b * nb + jnp.maximum(n * nq - 1, 0), col))
        own = pl.BlockSpec((nq * blk, kv_w), lambda b, n: (b * steps + n, col))
        nxt = pl.BlockSpec((blk, kv_w), lambda b, n: (b * nb + jnp.minimum((n + 1) * nq, nb - 1), col))
        return [prev, own, nxt]

    ya_lat = pl.pallas_call(
        functools.partial(_attn_lat_kernel, nb=nb, q_per_kv=q_per_kv, scale=None),
        out_shape=jax.ShapeDtypeStruct((bsz * s_len, attn_w), BF16),
        grid=(bsz, steps),
        in_specs=[pl.BlockSpec((nq * blk, attn_w), lambda b, n: (b * steps + n, 0))]
        + kv_specs(k_col) + kv_specs(v_col) + [
            pl.BlockSpec((c_len, kv_w), lambda b, n: (b, k_col)),
            pl.BlockSpec((c_len, kv_w), lambda b, n: (b, v_col)),
            pl.BlockSpec((n_q_heads, LANES), lambda b, n: (0, 0)),
        ],
        out_specs=pl.BlockSpec((nq * blk, attn_w), lambda b, n: (b * steps + n, 0)),
        compiler_params=_cparams(("parallel", "parallel"), 16 * 1024 * 1024),
        name="attn_latent",
    )(p_lat, p_lat, p_lat, p_lat, p_lat, p_lat, p_lat, p_ctx, p_ctx, sink_b)
    if not need_ctx:
        return ya_lat, None
    ya_ctx = pl.pallas_call(
        functools.partial(_attn_ctx_kernel, q_per_kv=q_per_kv, scale=scale),
        out_shape=jax.ShapeDtypeStruct((bsz * c_len, attn_w), BF16),
        grid=(bsz,),
        in_specs=[
            pl.BlockSpec((c_len, attn_w), lambda b: (b, 0)),
            pl.BlockSpec((c_len, kv_w), lambda b: (b, k_col)),
            pl.BlockSpec((c_len, kv_w), lambda b: (b, v_col)),
            pl.BlockSpec((n_q_heads, LANES), lambda b: (0, 0)),
        ],
        out_specs=pl.BlockSpec((c_len, attn_w), lambda b: (b, 0)),
        compiler_params=_cparams(("parallel",), 16 * 1024 * 1024),
        name="attn_context",
    )(p_ctx, p_ctx, p_ctx, sink_b)
    return ya_lat, ya_ctx


def _rglru_kernel(xp_ref, x_ref, xn_ref, cw_ref, cb_ref, w_ref, b_ref, lam_ref, h0_ref, *refs,
                  reverse, nk):
    if reverse:
        hf_ref, y_ref, hend_ref, xt_ref, h_ref, hs_ref = refs
    else:
        y_ref, hend_ref, xt_ref, h_ref = refs
    k = pl.program_id(1)
    ci = nk - 1 - k if reverse else k
    bsz, tc, width = x_ref.shape
    n_lb = width // LANES
    halo = xp_ref.shape[1]
    n_sub = tc // SCAN_SUB
    rows_sub = SCAN_SUB * bsz

    @pl.when(k == 0)
    def _():
        h_ref[...] = h0_ref[...]

    for c in range(n_lb):
        lanes = slice(c * LANES, (c + 1) * LANES)
        for b in range(bsz):
            xt_ref[c, pl.ds(b, halo, stride=bsz), :] = xp_ref[b, :, lanes].astype(F32)
            xt_ref[c, pl.ds(halo * bsz + b, tc, stride=bsz), :] = x_ref[b, :, lanes].astype(F32)
            xt_ref[c, pl.ds((halo + tc) * bsz + b, halo, stride=bsz), :] = xn_ref[b, :, lanes].astype(F32)

    @pl.when(ci == 0)
    def _():
        xt_ref[:, 0:halo * bsz, :] = jnp.zeros((n_lb, halo * bsz, LANES), F32)

    @pl.when(ci == nk - 1)
    def _():
        xt_ref[:, (halo + tc) * bsz:(2 * halo + tc) * bsz, :] = jnp.zeros((n_lb, halo * bsz, LANES), F32)

    def sub_body(i, hs):
        si = (n_sub - 1 - i) if reverse else i
        t0 = si * SCAN_SUB
        new_h = []
        for c in range(n_lb):
            lanes = slice(c * LANES, (c + 1) * LANES)
            cw = cw_ref[:, lanes]
            u = cb_ref[:, lanes]
            for kk in range(cw.shape[0]):
                r0 = pl.multiple_of((t0 + halo - CONV_LEFT + kk) * bsz, bsz)
                u = u + cw[kk:kk + 1, :] * xt_ref[c, pl.ds(r0, rows_sub), :]
            g = jnp.dot(u.astype(BF16), w_ref[c], preferred_element_type=F32)
            gr = 1.0 / (1.0 + jnp.exp2(g[:, :LANES] - LOG2_E * b_ref[0:1, lanes]))
            gi = 1.0 / (1.0 + jnp.exp2(g[:, LANES:] - LOG2_E * b_ref[1:2, lanes]))
            neg = -lam_ref[:, lanes]
            softplus = jnp.maximum(neg, 0.0) + jnp.log1p(jnp.exp(-jnp.abs(neg)))
            log_a = gr * ((-RGLRU_C) * softplus)
            a = jnp.exp(log_a)
            x1 = -jnp.tanh(log_a) * (a * a + 1.0)
            m = jnp.where(x1 > 0.0, x1 * lax.rsqrt(x1), 0.0)
            bx = m * (gi * u)
            h = hs[c]
            outs = [None] * SCAN_SUB
            for t in (range(SCAN_SUB - 1, -1, -1) if reverse else range(SCAN_SUB)):
                h = a[t * bsz:(t + 1) * bsz] * h + bx[t * bsz:(t + 1) * bsz]
                outs[t] = h
            rows = pl.ds(pl.multiple_of(t0 * bsz, rows_sub), rows_sub)
            if reverse:
                hs_ref[c, rows, :] = jnp.concatenate(outs, axis=0) + hf_ref[rows, lanes]
            else:
                y_ref[rows, lanes] = jnp.concatenate(outs, axis=0)
            new_h.append(h)
        return tuple(new_h)

    h_init = tuple(h_ref[:, c * LANES:(c + 1) * LANES] for c in range(n_lb))
    h_fin = lax.fori_loop(0, n_sub, sub_body, h_init)
    for c in range(n_lb):
        lanes = slice(c * LANES, (c + 1) * LANES)
        h_ref[:, lanes] = h_fin[c]
        if reverse:
            for b in range(bsz):
                y_ref[b, :, lanes] = hs_ref[c, pl.ds(b, tc, stride=bsz), :].astype(y_ref.dtype)

    @pl.when(k == nk - 1)
    def _():
        hend_ref[...] = h_ref[...]


def _rglru_dir(p3, col0, layer, conv_w, conv_b, w_gates, b_gates, lam, h0, hf, reverse):
    direction = 1 if reverse else 0
    bsz, seq, _ = p3.shape
    rnn_w = conv_w.shape[-1]
    width = SCAN_WIDTH
    tc = min(SCAN_CHUNK, seq)
    nk = seq // tc
    n_lb = width // LANES
    cblk0 = col0 // width
    halo = 2 * SUBLANES
    hpt = tc // halo
    n_halo_blocks = seq // halo

    def tmap(k):
        return nk - 1 - k if reverse else k

    in_specs = [
        pl.BlockSpec((bsz, halo, width), lambda j, k: (0, jnp.maximum(tmap(k) * hpt - 1, 0), cblk0 + j)),
        pl.BlockSpec((bsz, tc, width), lambda j, k: (0, tmap(k), cblk0 + j)),
        pl.BlockSpec((bsz, halo, width),
                     lambda j, k: (0, jnp.minimum((tmap(k) + 1) * hpt, n_halo_blocks - 1), cblk0 + j)),
        pl.BlockSpec((conv_w.shape[0], width), lambda j, k: (0, j)),
        pl.BlockSpec((1, width), lambda j, k: (0, j)),
        pl.BlockSpec((None, None, n_lb, LANES, 2 * LANES), lambda j, k: (layer, direction, j, 0, 0)),
        pl.BlockSpec((2, width), lambda j, k: (0, j)),
        pl.BlockSpec((1, width), lambda j, k: (0, j)),
        pl.BlockSpec((bsz, width), lambda j, k: (0, j)),
    ]
    args = [p3, p3, p3, conv_w, conv_b.reshape(1, rnn_w), w_gates, b_gates, lam.reshape(1, rnn_w), h0]
    time_major = pl.BlockSpec((tc * bsz, width), lambda j, k: (tmap(k), j))
    scratch = [pltpu.VMEM((n_lb, (tc + 2 * halo) * bsz, LANES), F32), pltpu.VMEM((bsz, width), F32)]
    if reverse:
        in_specs.append(time_major)
        args.append(hf)
        y_shape = jax.ShapeDtypeStruct((bsz, seq, rnn_w), BF16)
        y_spec = pl.BlockSpec((bsz, tc, width), lambda j, k: (0, tmap(k), j))
        scratch.append(pltpu.VMEM((n_lb, tc * bsz, LANES), F32))
    else:
        y_shape = jax.ShapeDtypeStruct((seq * bsz, rnn_w), F32)
        y_spec = time_major
    block_bytes = (2 * bsz * tc * width * 2 + 2 * bsz * tc * width * 4 * 2
                   + (tc + 2 * halo) * bsz * width * 4 + tc * bsz * width * 4)
    return pl.pallas_call(
        functools.partial(_rglru_kernel, reverse=reverse, nk=nk),
        out_shape=(y_shape, jax.ShapeDtypeStruct((bsz, rnn_w), F32)),
        grid=(rnn_w // width, nk),
        in_specs=in_specs,
        out_specs=(y_spec, pl.BlockSpec((bsz, width), lambda j, k: (0, j))),
        scratch_shapes=scratch,
        compiler_params=_cparams(("parallel", "arbitrary"), block_bytes),
        name="rglru_bwd" if reverse else "rglru_fwd",
    )(*args)


def _rglru(p3, col0, layer, conv_w, conv_b, w_gates, b_gates, lam, h0):
    hf, end_f = _rglru_dir(p3, col0, layer, conv_w, conv_b, w_gates, b_gates[0], lam[0], h0[0], None, False)
    y, end_b = _rglru_dir(p3, col0, layer, conv_w, conv_b, w_gates, b_gates[1], lam[1], h0[1], hf, True)
    return y, (end_f, end_b)


def _gelu_tanh(x):
    return 0.5 * x * (1.0 + jnp.tanh(0.7978845608028654 * (x + 0.044715 * (x * x * x))))


def _merge_kernel(x_ref, ya_ref, r_ref, gr0_ref, gr1_ref, ga0_ref, ga1_ref, gb0_ref, gb1_ref, gt_ref,
                  woa_ref, wor_ref, wout_ref, o_ref, yr_ref, acc_ref):
    c = pl.program_id(1)

    def wide(lo_ref, hi_ref):
        return jnp.concatenate([lo_ref[...], hi_ref[...]], axis=1).astype(F32)

    @pl.when(c == 0)
    def _():
        yr_ref[...] = (r_ref[...].astype(F32) * _gelu_tanh(wide(gr0_ref, gr1_ref))).astype(yr_ref.dtype)
        acc_ref[...] = jnp.zeros_like(acc_ref)

    pa = jnp.dot(ya_ref[...], woa_ref[...], preferred_element_type=F32)
    pr = jnp.dot(yr_ref[...], wor_ref[...], preferred_element_type=F32)
    t = jax.nn.sigmoid(wide(ga0_ref, ga1_ref)) * pa + jax.nn.sigmoid(wide(gb0_ref, gb1_ref)) * pr
    acc_ref[...] += jnp.dot(t.astype(BF16), wout_ref[...], preferred_element_type=F32)

    @pl.when(c == pl.num_programs(1) - 1)
    def _():
        o_ref[...] = x_ref[...] + gt_ref[...] * acc_ref[...]


def _merge(xs, ya, r, proj, gr_off, ga_off, gb_off, mods, layer, woa, wor, wout, row_of_tile):
    rows, d = xs.shape
    attn_w = ya.shape[1]
    rnn_w = r.shape[1]
    tm, tc = TM_MERGE, TC_MERGE
    half = tc // 2
    assert rnn_w == tc and gr_off % half == 0 and ga_off % half == 0 and gb_off % half == 0

    def half_spec(off, k):
        return pl.BlockSpec((tm, half), lambda i, c: (i, off // half + 2 * c + k))

    def gr_spec(k):
        return pl.BlockSpec((tm, half), lambda i, c: (i, gr_off // half + k))

    block_bytes = (4 * tm * d * 4 + 2 * tm * (attn_w + 2 * rnn_w + 2 * tc) * 2
                   + 2 * (attn_w + rnn_w + d) * tc * 2 + tm * rnn_w * 2 + tm * d * 4 + 3 * tm * tc * 4)
    return pl.pallas_call(
        _merge_kernel,
        out_shape=jax.ShapeDtypeStruct((rows, d), F32),
        grid=(rows // tm, d // tc),
        in_specs=[
            pl.BlockSpec((tm, d), lambda i, c: (i, 0)),
            pl.BlockSpec((tm, attn_w), lambda i, c: (i, 0)),
            pl.BlockSpec((tm, rnn_w), lambda i, c: (i, 0)),
            gr_spec(0), gr_spec(1),
            half_spec(ga_off, 0), half_spec(ga_off, 1),
            half_spec(gb_off, 0), half_spec(gb_off, 1),
            _mod_spec(layer, 5, row_of_tile, d),
            pl.BlockSpec((None, attn_w, tc), lambda i, c: (layer, 0, c)),
            pl.BlockSpec((None, rnn_w, tc), lambda i, c: (layer, 0, c)),
            pl.BlockSpec((None, tc, d), lambda i, c: (layer, c, 0)),
        ],
        out_specs=pl.BlockSpec((tm, d), lambda i, c: (i, 0)),
        scratch_shapes=[pltpu.VMEM((tm, rnn_w), BF16), pltpu.VMEM((tm, d), F32)],
        input_output_aliases={0: 0},
        compiler_params=_cparams(("parallel", "arbitrary"), block_bytes),
        name="merge",
    )(xs, ya, r, proj, proj, proj, proj, proj, proj, mods, woa, wor, wout)


def _up_cast_kernel(w_ref, g_ref, u_ref, *, d_ff):
    pad = g_ref.shape[1] - d_ff
    g_ref[:, :d_ff] = w_ref[:, :d_ff].astype(g_ref.dtype)
    u_ref[:, :d_ff] = w_ref[:, d_ff:].astype(u_ref.dtype)
    if pad:
        g_ref[:, d_ff:] = jnp.zeros((g_ref.shape[0], pad), g_ref.dtype)
        u_ref[:, d_ff:] = jnp.zeros((u_ref.shape[0], pad), u_ref.dtype)


def _ffn_up_weights(w_up, d_ff, ff_pad):
    depth, halves, d, n = w_up.shape
    assert n == 2 * d_ff and d_ff % LANES == 0 and d % CAST_ROWS == 0
    out = jax.ShapeDtypeStruct((depth * halves, d, ff_pad), BF16)
    wg, wu = pl.pallas_call(
        functools.partial(_up_cast_kernel, d_ff=d_ff),
        out_shape=(out, out),
        grid=(depth * halves, d // CAST_ROWS),
        in_specs=[pl.BlockSpec((None, CAST_ROWS, n), lambda l, r: (l, r, 0))],
        out_specs=(pl.BlockSpec((None, CAST_ROWS, ff_pad), lambda l, r: (l, r, 0)),
                   pl.BlockSpec((None, CAST_ROWS, ff_pad), lambda l, r: (l, r, 0))),
        compiler_params=_cparams(("parallel", "parallel"), 2 * CAST_ROWS * (n * 4 + 2 * ff_pad * 2)),
        name="ffn_up_cast",
    )(w_up.reshape(depth * halves, d, n))
    return wg.reshape(depth, halves, d, ff_pad), wu.reshape(depth, halves, d, ff_pad)


def _down_cast_kernel(w_ref, o_ref, *, tail_rows):
    r = pl.program_id(1)
    last = pl.num_programs(1) - 1

    @pl.when(r != last)
    def _():
        o_ref[...] = w_ref[...].astype(o_ref.dtype)

    @pl.when(r == last)
    def _():
        o_ref[:tail_rows, :] = w_ref[:tail_rows, :].astype(o_ref.dtype)
        if tail_rows < o_ref.shape[0]:
            o_ref[tail_rows:, :] = jnp.zeros((o_ref.shape[0] - tail_rows, o_ref.shape[1]), o_ref.dtype)


def _ffn_down_weights(w_down, ff_pad):
    depth, halves, d_ff, d = w_down.shape
    n_blocks = ff_pad // TF
    tail_rows = d_ff - (n_blocks - 1) * TF
    assert 0 < tail_rows <= TF and tail_rows % (2 * SUBLANES) == 0
    wd = pl.pallas_call(
        functools.partial(_down_cast_kernel, tail_rows=tail_rows),
        out_shape=jax.ShapeDtypeStruct((depth * halves, ff_pad, d), BF16),
        grid=(depth * halves, n_blocks),
        in_specs=[pl.BlockSpec((None, TF, d), lambda l, r: (l, r, 0))],
        out_specs=pl.BlockSpec((None, TF, d), lambda l, r: (l, r, 0)),
        compiler_params=_cparams(("parallel", "parallel"), 2 * TF * d * (4 + 2)),
        name="ffn_down_cast",
    )(w_down.reshape(depth * halves, d_ff, d))
    return wd.reshape(depth, halves, ff_pad, d)


def _rope_tables(s_len):
    pos = jnp.arange(s_len, dtype=jnp.int32)
    row = (pos // GRID_W).astype(F32)
    col = (pos % GRID_W).astype(F32)
    inv_freq = ROPE_BASE ** (-jnp.arange(ROPE_FREQS, dtype=F32) / ROPE_FREQS)
    ang_r = row[:, None] * inv_freq
    ang_c = col[:, None] * inv_freq
    cos_t = jnp.concatenate([jnp.cos(ang_r), jnp.cos(ang_r), jnp.cos(ang_c), jnp.cos(ang_c)], axis=1)
    sin_t = jnp.concatenate([-jnp.sin(ang_r), jnp.sin(ang_r), -jnp.sin(ang_c), jnp.sin(ang_c)], axis=1)
    return cos_t, sin_t


def kernel(x, c, ctx, c_ctx, w_ada, b_ada, norm_g, final_g, w_ffn_up, w_ffn_down, w_in, attn_sink,
           conv_w, conv_b, rg_w, rg_b, rg_lambda, w_o_attn, w_o_rnn, w_out):
    bsz, s_len, d = x.shape
    c_len = ctx.shape[1]
    depth = w_ada.shape[0]
    d_ff = w_ffn_down.shape[2]
    attn_w = w_o_attn.shape[1]
    rnn_w = w_o_rnn.shape[1]
    kv_w = N_KV_HEADS * HEAD_DIM
    ff_pad = -(-d_ff // TF) * TF
    n_lat, n_ctx = bsz * s_len, bsz * c_len
    assert s_len % TM_FFN == 0 and n_ctx % TM_FFN == 0 and s_len % TM_MERGE == 0 and n_ctx % TM_MERGE == 0
    assert bsz == SUBLANES and bsz + 1 <= MOD_ROWS and rnn_w % SCAN_WIDTH == 0
    assert s_len % SCAN_CHUNK == 0 and c_len % SCAN_SUB == 0 and c_len <= SCAN_CHUNK

    off_k = attn_w
    off_xr = off_k + 2 * kv_w
    off_gr = off_xr + rnn_w
    off_ga = off_gr + rnn_w
    off_gb = off_ga + d
    assert off_xr == TN_PROJ and off_k % kv_w == 0 and off_xr % SCAN_WIDTH == 0

    def lat_row(tm):
        return lambda i: i // (s_len // tm)

    def ctx_row(i):
        return bsz

    wg_all, wu_all = _ffn_up_weights(w_ffn_up, d_ff, ff_pad)
    wd_all = _ffn_down_weights(w_ffn_down, ff_pad)
    w_in_all = w_in.astype(BF16)
    woa_all, wor_all, wout_all = w_o_attn.astype(BF16), w_o_rnn.astype(BF16), w_out.astype(BF16)
    w_gates_all = (-LOG2_E * jnp.concatenate([rg_w[:, :, 0], rg_w[:, :, 1]], axis=-1)).astype(BF16)

    c_all = jnp.zeros((MOD_ROWS, d), F32).at[:bsz].set(c).at[bsz].set(c_ctx)
    mods = _adaln(c_all, w_ada, b_ada).reshape(depth, MOD_ROWS, 1, N_MOD * d)
    cos_t, sin_t = _rope_tables(s_len)
    rope = (cos_t, sin_t, attn_w + kv_w, attn_w, HEAD_DIM ** -0.5 * LOG2_E, s_len // TM_FFN)
    xl = x.reshape(n_lat, d)
    xc = ctx.reshape(n_ctx, d)
    h_zero = (jnp.zeros((bsz, rnn_w), F32), jnp.zeros((bsz, rnn_w), F32))

    for l in range(depth):
        need_ctx = l < depth - 1
        first = l == 0
        xl = _ffn(xl, mods, l, 0, norm_g[l, 0], wg_all, wu_all, wd_all, lat_row(TM_FFN), in_place=not first)
        xc = _ffn(xc, mods, l, 0, norm_g[l, 0], wg_all, wu_all, wd_all, ctx_row, in_place=not first)

        p_lat = _proj(xl, mods, l, norm_g[l, 1], w_in_all, lat_row(TM_FFN), rope=rope)
        p_ctx = _proj(xc, mods, l, norm_g[l, 1], w_in_all, ctx_row)

        sink_b = jnp.broadcast_to(attn_sink[l][:, None], (attn_sink.shape[1], LANES))
        ya_lat, ya_ctx = _attention(p_lat, p_ctx, sink_b, bsz, s_len, c_len, attn_w, off_k, need_ctx)

        r_ctx, h_end = _rglru(p_ctx.reshape(bsz, c_len, -1), off_xr, l, conv_w[l], conv_b[l], w_gates_all,
                              rg_b[l], rg_lambda[l], h_zero)
        r_lat, _ = _rglru(p_lat.reshape(bsz, s_len, -1), off_xr, l, conv_w[l], conv_b[l], w_gates_all,
                          rg_b[l], rg_lambda[l], h_end)

        xl = _merge(xl, ya_lat, r_lat.reshape(n_lat, rnn_w), p_lat, off_gr, off_ga, off_gb, mods, l,
                    woa_all, wor_all, wout_all, lat_row(TM_MERGE))
        if need_ctx:
            xc = _merge(xc, ya_ctx, r_ctx.reshape(n_ctx, rnn_w), p_ctx, off_gr, off_ga, off_gb, mods, l,
                        woa_all, wor_all, wout_all, ctx_row)
            xc = _ffn(xc, mods, l, 1, norm_g[l, 2], wg_all, wu_all, wd_all, ctx_row)
        xl = _ffn(xl, mods, l, 1, norm_g[l, 2], wg_all, wu_all, wd_all, lat_row(TM_FFN),
                  final_g=None if need_ctx else final_g)

    return xl.reshape(bsz, s_len, d)
```

```python
import functools

import jax
import jax.numpy as jnp
from jax import lax
from jax.experimental import pallas as pl
from jax.experimental.pallas import tpu as pltpu

F32 = jnp.float32
BF16 = jnp.bfloat16

LANES = 128
SUBLANES = 8
VMEM_PHYSICAL_BYTES = 64 * 1024 * 1024
VMEM_HEADROOM_BYTES = 4 * 1024 * 1024

HEAD_DIM = 128
N_KV_HEADS = 2
WINDOW_BLOCK = 128
GRID_W = 64
ROPE_BASE = 10000.0
ROPE_FREQS = HEAD_DIM // 4
CONV_LEFT = 2
RGLRU_C = 8.0
N_MOD = 9
EPS = 1e-6
MOD_ROWS = 16
NEG_BIG = -1e30
LOG2_E = 1.4426950408889634

TM_FFN = 1024
TF = 512
TN_PROJ = 1536
TM_MERGE = 512
TC_MERGE = 1024
TN_ADA = 1024
NORM_ROWS = 128
NORM_UNROLL = 4
SCAN_CHUNK = 256
SCAN_SUB = 64
SCAN_WIDTH = 512
ATTN_Q_BLOCKS = 8
CAST_ROWS = 256


def _cparams(sem, block_bytes):
    limit = min(int(block_bytes) + VMEM_HEADROOM_BYTES, VMEM_PHYSICAL_BYTES - VMEM_HEADROOM_BYTES // 2)
    return pltpu.CompilerParams(dimension_semantics=sem, vmem_limit_bytes=limit)


def _adaln_kernel(c_ref, w_ref, b_ref, o_ref):
    c = c_ref[...]
    sc = c * jax.nn.sigmoid(c)
    o_ref[...] = jnp.dot(sc, w_ref[...], preferred_element_type=F32,
                         precision=lax.Precision.HIGHEST) + b_ref[...]


def _adaln(c_all, w_ada, b_ada):
    depth, d, n = w_ada.shape
    return pl.pallas_call(
        _adaln_kernel,
        out_shape=jax.ShapeDtypeStruct((depth, MOD_ROWS, n), F32),
        grid=(depth, n // TN_ADA),
        in_specs=[
            pl.BlockSpec((MOD_ROWS, d), lambda l, j: (0, 0)),
            pl.BlockSpec((None, d, TN_ADA), lambda l, j: (l, 0, j)),
            pl.BlockSpec((None, 1, TN_ADA), lambda l, j: (l, 0, j)),
        ],
        out_specs=pl.BlockSpec((None, MOD_ROWS, TN_ADA), lambda l, j: (l, 0, j)),
        compiler_params=_cparams(("parallel", "parallel"), 2 * d * TN_ADA * 4 + 4 * MOD_ROWS * (d + TN_ADA) * 4),
        name="adaln",
    )(c_all, w_ada, b_ada.reshape(depth, 1, n))


def _rms_scale(x):
    return x * lax.rsqrt(jnp.mean(x * x, axis=-1, keepdims=True) + EPS)


def _norm_mod_store(xn_ref, rs_ref, x_ref, g_ref, sh_ref, sc_ref, zero_ref=None):
    n_steps = x_ref.shape[0] // NORM_ROWS
    inv_d = 1.0 / x_ref.shape[1]
    gs = g_ref[...] * (1.0 + sc_ref[...])
    sh = sh_ref[...]

    def stats(i, carry):
        r0 = pl.multiple_of(i * NORM_ROWS, NORM_ROWS)
        x = x_ref[pl.ds(r0, NORM_ROWS), :]
        rs_ref[pl.ds(r0, NORM_ROWS), :] = lax.rsqrt(jnp.sum(x * x, axis=-1, keepdims=True) * inv_d + EPS)
        return carry

    lax.fori_loop(0, n_steps, stats, 0, unroll=NORM_UNROLL)

    def scale(i, carry):
        r0 = pl.multiple_of(i * NORM_ROWS, NORM_ROWS)
        y = (x_ref[pl.ds(r0, NORM_ROWS), :] * rs_ref[pl.ds(r0, NORM_ROWS), :]) * gs + sh
        xn_ref[pl.ds(r0, NORM_ROWS), :] = y.astype(xn_ref.dtype)
        if zero_ref is not None:
            zero_ref[pl.ds(r0, NORM_ROWS), :] = jnp.zeros((NORM_ROWS, zero_ref.shape[1]), zero_ref.dtype)
        return carry

    lax.fori_loop(0, n_steps, scale, 0)


def _mod_spec(layer, k, row_of_tile, d):
    return pl.BlockSpec((None, None, 1, d), lambda i, j: (layer, row_of_tile(i), 0, k))


def _ffn_kernel(x_ref, g_ref, sh_ref, sc_ref, gt_ref, wg_ref, wu_ref, wd_ref, *refs, final_norm):
    if final_norm:
        fg_ref, o_ref, xn_ref, rs_ref = refs
    else:
        o_ref, xn_ref, rs_ref = refs
    c = pl.program_id(1)

    @pl.when(c == 0)
    def _():
        _norm_mod_store(xn_ref, rs_ref, x_ref, g_ref, sh_ref, sc_ref, zero_ref=o_ref)

    xn = xn_ref[...]
    h = jnp.dot(xn, wg_ref[...], preferred_element_type=F32)
    u = jnp.dot(xn, wu_ref[...], preferred_element_type=F32)
    a = ((h * jax.nn.sigmoid(h)) * u).astype(BF16)
    o_ref[...] += jnp.dot(a, wd_ref[...], preferred_element_type=F32)

    @pl.when(c == pl.num_programs(1) - 1)
    def _():
        gate = 0.5 * gt_ref[...]

        def body(i, carry):
            r0 = pl.multiple_of(i * NORM_ROWS, NORM_ROWS)
            y = x_ref[pl.ds(r0, NORM_ROWS), :] + gate * o_ref[pl.ds(r0, NORM_ROWS), :]
            if final_norm:
                y = _rms_scale(y) * fg_ref[...]
            o_ref[pl.ds(r0, NORM_ROWS), :] = y
            return carry

        lax.fori_loop(0, x_ref.shape[0] // NORM_ROWS, body, 0)


def _ffn(xs, mods, layer, half, g, wg, wu, wd, row_of_tile, final_g=None, in_place=True):
    rows, d = xs.shape
    ff = wg.shape[-1]
    tm = TM_FFN
    mod_k0 = 6 * half
    final_norm = final_g is not None
    in_specs = [
        pl.BlockSpec((tm, d), lambda i, c: (i, 0)),
        pl.BlockSpec((1, d), lambda i, c: (0, 0)),
        _mod_spec(layer, mod_k0, row_of_tile, d),
        _mod_spec(layer, mod_k0 + 1, row_of_tile, d),
        _mod_spec(layer, mod_k0 + 2, row_of_tile, d),
        pl.BlockSpec((None, None, d, TF), lambda i, c: (layer, half, 0, c)),
        pl.BlockSpec((None, None, d, TF), lambda i, c: (layer, half, 0, c)),
        pl.BlockSpec((None, None, TF, d), lambda i, c: (layer, half, c, 0)),
    ]
    args = [xs, g.reshape(1, d), mods, mods, mods, wg, wu, wd]
    if final_norm:
        in_specs.append(pl.BlockSpec((1, d), lambda i, c: (0, 0)))
        args.append(final_g.reshape(1, d))
    block_bytes = (2 * tm * d * 4 + 2 * tm * d * 4 + 2 * 3 * d * TF * 2 + tm * d * 2
                   + 2 * tm * TF * 4 + tm * TF * 2)
    return pl.pallas_call(
        functools.partial(_ffn_kernel, final_norm=final_norm),
        out_shape=jax.ShapeDtypeStruct((rows, d), F32),
        grid=(rows // tm, ff // TF),
        in_specs=in_specs,
        out_specs=pl.BlockSpec((tm, d), lambda i, c: (i, 0)),
        scratch_shapes=[pltpu.VMEM((tm, d), BF16), pltpu.VMEM((tm, 1), F32)],
        input_output_aliases={0: 0} if (in_place and not final_norm) else {},
        compiler_params=_cparams(("parallel", "arbitrary"), block_bytes),
        name="ffn_final" if final_norm else "ffn",
    )(*args)


def _rope(t, cs, sn):
    lane = lax.broadcasted_iota(jnp.int32, t.shape, 1)
    first_half = (lane & (2 * ROPE_FREQS - 1)) < ROPE_FREQS
    partner = jnp.where(first_half,
                        pltpu.roll(t, HEAD_DIM - ROPE_FREQS, axis=1),
                        pltpu.roll(t, ROPE_FREQS, axis=1))
    return t * cs + partner * sn


def _proj_kernel(x_ref, g_ref, sh_ref, sc_ref, w_ref, *refs, rope_cols, q_cols, q_scale):
    if rope_cols:
        cos_ref, sin_ref, o_ref, xn_ref, rs_ref = refs
    else:
        o_ref, xn_ref, rs_ref = refs
    j = pl.program_id(1)

    @pl.when(j == 0)
    def _():
        _norm_mod_store(xn_ref, rs_ref, x_ref, g_ref, sh_ref, sc_ref)

    def plain():
        o_ref[...] = jnp.dot(xn_ref[...], w_ref[...], preferred_element_type=F32).astype(o_ref.dtype)

    if not rope_cols:
        plain()
        return

    @pl.when(j == 0)
    def _():
        xn = xn_ref[...]
        cs = cos_ref[...]
        sn = sin_ref[...]
        pair = 2 * HEAD_DIM
        for c0 in range(0, o_ref.shape[1], pair):
            res = jnp.dot(xn, w_ref[:, c0:c0 + pair], preferred_element_type=F32)
            for h0 in range(0, pair, HEAD_DIM):
                t = res[:, h0:h0 + HEAD_DIM]
                if c0 + h0 < rope_cols:
                    t = _rope(t, cs, sn)
                if c0 + h0 < q_cols:
                    t = t * q_scale
                o_ref[:, c0 + h0:c0 + h0 + HEAD_DIM] = t.astype(o_ref.dtype)

    pl.when(j != 0)(plain)


def _proj(xs, mods, layer, g, w, row_of_tile, rope=None):
    rows, d = xs.shape
    n = w.shape[-1]
    tm = TM_FFN
    in_specs = [
        pl.BlockSpec((tm, d), lambda i, j: (i, 0)),
        pl.BlockSpec((1, d), lambda i, j: (0, 0)),
        _mod_spec(layer, 3, row_of_tile, d),
        _mod_spec(layer, 4, row_of_tile, d),
        pl.BlockSpec((None, d, TN_PROJ), lambda i, j: (layer, 0, j)),
    ]
    args = [xs, g.reshape(1, d), mods, mods, w]
    rope_cols = q_cols = 0
    q_scale = 1.0
    if rope is not None:
        cos_t, sin_t, rope_cols, q_cols, q_scale, tiles_per_sample = rope
        assert q_cols <= rope_cols <= TN_PROJ
        in_specs += [pl.BlockSpec((tm, HEAD_DIM), lambda i, j: (i % tiles_per_sample, 0))] * 2
        args += [cos_t, sin_t]
    block_bytes = (2 * tm * d * 4 + tm * d * 2 + 2 * d * TN_PROJ * 2 + 2 * tm * TN_PROJ * 2
                   + tm * TN_PROJ * 4 + 4 * tm * HEAD_DIM * 4)
    return pl.pallas_call(
        functools.partial(_proj_kernel, rope_cols=rope_cols, q_cols=q_cols, q_scale=q_scale),
        out_shape=jax.ShapeDtypeStruct((rows, n), BF16),
        grid=(rows // tm, n // TN_PROJ),
        in_specs=in_specs,
        out_specs=pl.BlockSpec((tm, TN_PROJ), lambda i, j: (i, j)),
        scratch_shapes=[pltpu.VMEM((tm, d), BF16), pltpu.VMEM((tm, 1), F32)],
        compiler_params=_cparams(("parallel", "arbitrary"), block_bytes),
        name="in_proj_rope" if rope_cols else "in_proj",
    )(*args)


def _scores(q4, kcat):
    return lax.dot_general(q4, kcat, (((1,), (1,)), ((), ())), preferred_element_type=F32)


def _softmax_pv(s, vcat, sk, bias, scale):
    if scale is None:
        sk = sk * LOG2_E
        exp = jnp.exp2
    else:
        s = s * scale
        exp = jnp.exp
    if bias is not None:
        rb = bias.shape[0]
        s = jnp.concatenate([s[r0:r0 + rb] + bias for r0 in range(0, s.shape[0], rb)], axis=0)
    m = jnp.maximum(jnp.max(s, axis=-1, keepdims=True), sk)
    p = exp(s - m)
    denom = jnp.sum(p, axis=-1, keepdims=True) + exp(sk - m)
    o = jnp.dot(p.astype(BF16), vcat, preferred_element_type=F32)
    return o / denom


def _sink_rows(sink_ref, g, q_per_kv, rows):
    parts = [jnp.broadcast_to(sink_ref[g * q_per_kv + j:g * q_per_kv + j + 1, 0:1], (rows, 1))
             for j in range(q_per_kv)]
    return jnp.concatenate(parts, axis=0)


def _attn_lat_kernel(q_ref, kp_ref, kc_ref, kn_ref, vp_ref, vc_ref, vn_ref, kx_ref, vx_ref,
                     sink_ref, o_ref, *, nb, q_per_kv, scale):
    n = pl.program_id(1)
    blk = WINDOW_BLOCK
    dh = HEAD_DIM
    c_len = kx_ref.shape[0]
    nq = q_ref.shape[0] // blk
    keys = 3 * blk + c_len
    ri = lax.broadcasted_iota(jnp.int32, (blk, keys), 0)
    kj = lax.broadcasted_iota(jnp.int32, (blk, keys), 1)
    band = (kj >= 3 * blk) | ((kj >= ri) & (kj <= ri + 2 * blk))

    def window(qb, p_ref, c_ref, n_ref, x_ref, cols):
        parts = []
        for w in (qb - 1, qb, qb + 1):
            if w < 0:
                parts.append(p_ref[:, cols])
            elif w >= nq:
                parts.append(n_ref[:, cols])
            else:
                parts.append(c_ref[w * blk:(w + 1) * blk, cols])
        return jnp.concatenate(parts + [x_ref[:, cols]], axis=0)

    def scores(qb, g):
        q4 = jnp.concatenate([q_ref[qb * blk:(qb + 1) * blk, (g * q_per_kv + j) * dh:(g * q_per_kv + j + 1) * dh]
                              for j in range(q_per_kv)], axis=0)
        return _scores(q4, window(qb, kp_ref, kc_ref, kn_ref, kx_ref, slice(g * dh, (g + 1) * dh)))

    streams = [(qb, g) for qb in range(nq) for g in range(N_KV_HEADS)]
    s_next = scores(*streams[0])
    for i, (qb, g) in enumerate(streams):
        s_cur = s_next
        if i + 1 < len(streams):
            s_next = scores(*streams[i + 1])
        first = (n == 0) if qb == 0 else False
        last = (n == nb // nq - 1) if qb == nq - 1 else False
        lo = jnp.where(first, blk, 0)
        hi = jnp.where(last, 2 * blk, 3 * blk)
        bias = jnp.where(band & ((kj >= 3 * blk) | ((kj >= lo) & (kj < hi))), 0.0, NEG_BIG).astype(F32)
        vcat = window(qb, vp_ref, vc_ref, vn_ref, vx_ref, slice(g * dh, (g + 1) * dh))
        o = _softmax_pv(s_cur, vcat, _sink_rows(sink_ref, g, q_per_kv, blk), bias, scale)
        for j in range(q_per_kv):
            h = g * q_per_kv + j
            o_ref[qb * blk:(qb + 1) * blk, h * dh:(h + 1) * dh] = o[j * blk:(j + 1) * blk].astype(o_ref.dtype)


def _attn_ctx_kernel(q_ref, kx_ref, vx_ref, sink_ref, o_ref, *, q_per_kv, scale):
    dh = HEAD_DIM
    c_len = q_ref.shape[0]
    for g in range(N_KV_HEADS):
        cols = slice(g * dh, (g + 1) * dh)
        q4 = jnp.concatenate([q_ref[:, (g * q_per_kv + j) * dh:(g * q_per_kv + j + 1) * dh]
                              for j in range(q_per_kv)], axis=0)
        sk = _sink_rows(sink_ref, g, q_per_kv, c_len)
        o = _softmax_pv(_scores(q4, kx_ref[:, cols]), vx_ref[:, cols], sk, None, scale)
        for j in range(q_per_kv):
            h = g * q_per_kv + j
            o_ref[:, h * dh:(h + 1) * dh] = o[j * c_len:(j + 1) * c_len].astype(o_ref.dtype)


def _attention(p_lat, p_ctx, sink_b, bsz, s_len, c_len, attn_w, k_off, need_ctx):
    kv_w = N_KV_HEADS * HEAD_DIM
    n_q_heads = attn_w // HEAD_DIM
    q_per_kv = n_q_heads // N_KV_HEADS
    blk = WINDOW_BLOCK
    nb = s_len // blk
    k_col = k_off // kv_w
    v_col = k_col + 1
    scale = HEAD_DIM ** -0.5

    nq = ATTN_Q_BLOCKS
    steps = nb // nq
    assert nb % nq == 0

    def attn_bytes(q_rows):
        blocks = 2 * 2 * (2 * q_rows * attn_w + 2 * (q_rows + 2 * blk + c_len) * kv_w)
        return blocks + 8 * q_per_kv * max(blk, c_len) * (3 * blk + c_len) * 4

    def kv_specs(col):
        prev = pl.BlockSpec((blk, kv_w), lambda b, n: (b * nb + jnp.maximum(n * nq - 1, 0), col))
        own = pl.BlockSpec((nq * blk, kv_w), lambda b, n: (b * steps + n, col))
        nxt = pl.BlockSpec((blk, kv_w), lambda b, n: (b * nb + jnp.minimum((n + 1) * nq, nb - 1), col))
        return [prev, own, nxt]

    ya_lat = pl.pallas_call(
        functools.partial(_attn_lat_kernel, nb=nb, q_per_kv=q_per_kv, scale=None),
        out_shape=jax.ShapeDtypeStruct((bsz * s_len, attn_w), BF16),
        grid=(bsz, steps),
        in_specs=[pl.BlockSpec((nq * blk, attn_w), lambda b, n: (b * steps + n, 0))]
        + kv_specs(k_col) + kv_specs(v_col) + [
            pl.BlockSpec((c_len, kv_w), lambda b, n: (b, k_col)),
            pl.BlockSpec((c_len, kv_w), lambda b, n: (b, v_col)),
            pl.BlockSpec((n_q_heads, LANES), lambda b, n: (0, 0)),
        ],
        out_specs=pl.BlockSpec((nq * blk, attn_w), lambda b, n: (b * steps + n, 0)),
        compiler_params=_cparams(("parallel", "parallel"), attn_bytes(nq * blk)),
        name="attn_latent",
    )(p_lat, p_lat, p_lat, p_lat, p_lat, p_lat, p_lat, p_ctx, p_ctx, sink_b)
    if not need_ctx:
        return ya_lat, None
    ya_ctx = pl.pallas_call(
        functools.partial(_attn_ctx_kernel, q_per_kv=q_per_kv, scale=scale),
        out_shape=jax.ShapeDtypeStruct((bsz * c_len, attn_w), BF16),
        grid=(bsz,),
        in_specs=[
            pl.BlockSpec((c_len, attn_w), lambda b: (b, 0)),
            pl.BlockSpec((c_len, kv_w), lambda b: (b, k_col)),
            pl.BlockSpec((c_len, kv_w), lambda b: (b, v_col)),
            pl.BlockSpec((n_q_heads, LANES), lambda b: (0, 0)),
        ],
        out_specs=pl.BlockSpec((c_len, attn_w), lambda b: (b, 0)),
        compiler_params=_cparams(("parallel",), attn_bytes(c_len)),
        name="attn_context",
    )(p_ctx, p_ctx, p_ctx, sink_b)
    return ya_lat, ya_ctx


def _rglru_kernel(xp_ref, x_ref, xn_ref, cw_ref, cb_ref, w_ref, b_ref, lam_ref, h0_ref, *refs,
                  reverse, nk):
    if reverse:
        hf_ref, y_ref, hend_ref, xt_ref, h_ref, hs_ref = refs
    else:
        y_ref, hend_ref, xt_ref, h_ref = refs
    k = pl.program_id(1)
    ci = nk - 1 - k if reverse else k
    bsz, tc, width = x_ref.shape
    n_lb = width // LANES
    halo = xp_ref.shape[1]
    n_sub = tc // SCAN_SUB
    rows_sub = SCAN_SUB * bsz

    @pl.when(k == 0)
    def _():
        h_ref[...] = h0_ref[...]

    for c in range(n_lb):
        lanes = slice(c * LANES, (c + 1) * LANES)
        for b in range(bsz):
            xt_ref[c, pl.ds(b, halo, stride=bsz), :] = xp_ref[b, :, lanes].astype(F32)
            xt_ref[c, pl.ds(halo * bsz + b, tc, stride=bsz), :] = x_ref[b, :, lanes].astype(F32)
            xt_ref[c, pl.ds((halo + tc) * bsz + b, halo, stride=bsz), :] = xn_ref[b, :, lanes].astype(F32)

    @pl.when(ci == 0)
    def _():
        xt_ref[:, 0:halo * bsz, :] = jnp.zeros((n_lb, halo * bsz, LANES), F32)

    @pl.when(ci == nk - 1)
    def _():
        xt_ref[:, (halo + tc) * bsz:(2 * halo + tc) * bsz, :] = jnp.zeros((n_lb, halo * bsz, LANES), F32)

    def sub_body(i, hs):
        si = (n_sub - 1 - i) if reverse else i
        t0 = si * SCAN_SUB
        new_h = []
        for c in range(n_lb):
            lanes = slice(c * LANES, (c + 1) * LANES)
            cw = cw_ref[:, lanes]
            u = cb_ref[:, lanes]
            for kk in range(cw.shape[0]):
                r0 = pl.multiple_of((t0 + halo - CONV_LEFT + kk) * bsz, bsz)
                u = u + cw[kk:kk + 1, :] * xt_ref[c, pl.ds(r0, rows_sub), :]
            g = jnp.dot(u.astype(BF16), w_ref[c], preferred_element_type=F32)
            gr = 1.0 / (1.0 + jnp.exp2(g[:, :LANES] - LOG2_E * b_ref[0:1, lanes]))
            gi = 1.0 / (1.0 + jnp.exp2(g[:, LANES:] - LOG2_E * b_ref[1:2, lanes]))
            neg = -lam_ref[:, lanes]
            softplus = jnp.maximum(neg, 0.0) + jnp.log1p(jnp.exp(-jnp.abs(neg)))
            log_a = gr * ((-RGLRU_C) * softplus)
            a = jnp.exp(log_a)
            x1 = -jnp.tanh(log_a) * (a * a + 1.0)
            m = jnp.where(x1 > 0.0, x1 * lax.rsqrt(x1), 0.0)
            bx = m * (gi * u)
            h = hs[c]
            outs = [None] * SCAN_SUB
            for t in (range(SCAN_SUB - 1, -1, -1) if reverse else range(SCAN_SUB)):
                h = a[t * bsz:(t + 1) * bsz] * h + bx[t * bsz:(t + 1) * bsz]
                outs[t] = h
            rows = pl.ds(pl.multiple_of(t0 * bsz, rows_sub), rows_sub)
            if reverse:
                hs_ref[c, rows, :] = jnp.concatenate(outs, axis=0) + hf_ref[rows, lanes]
            else:
                y_ref[rows, lanes] = jnp.concatenate(outs, axis=0)
            new_h.append(h)
        return tuple(new_h)

    h_init = tuple(h_ref[:, c * LANES:(c + 1) * LANES] for c in range(n_lb))
    h_fin = lax.fori_loop(0, n_sub, sub_body, h_init)
    for c in range(n_lb):
        lanes = slice(c * LANES, (c + 1) * LANES)
        h_ref[:, lanes] = h_fin[c]
        if reverse:
            for b in range(bsz):
                y_ref[b, :, lanes] = hs_ref[c, pl.ds(b, tc, stride=bsz), :].astype(y_ref.dtype)

    @pl.when(k == nk - 1)
    def _():
        hend_ref[...] = h_ref[...]


def _rglru_dir(p3, col0, layer, conv_w, conv_b, w_gates, b_gates, lam, h0, hf, reverse):
    direction = 1 if reverse else 0
    bsz, seq, _ = p3.shape
    rnn_w = conv_w.shape[-1]
    width = SCAN_WIDTH
    tc = min(SCAN_CHUNK, seq)
    nk = seq // tc
    n_lb = width // LANES
    cblk0 = col0 // width
    halo = 2 * SUBLANES
    hpt = tc // halo
    n_halo_blocks = seq // halo

    def tmap(k):
        return nk - 1 - k if reverse else k

    in_specs = [
        pl.BlockSpec((bsz, halo, width), lambda j, k: (0, jnp.maximum(tmap(k) * hpt - 1, 0), cblk0 + j)),
        pl.BlockSpec((bsz, tc, width), lambda j, k: (0, tmap(k), cblk0 + j)),
        pl.BlockSpec((bsz, halo, width),
                     lambda j, k: (0, jnp.minimum((tmap(k) + 1) * hpt, n_halo_blocks - 1), cblk0 + j)),
        pl.BlockSpec((conv_w.shape[0], width), lambda j, k: (0, j)),
        pl.BlockSpec((1, width), lambda j, k: (0, j)),
        pl.BlockSpec((None, None, n_lb, LANES, 2 * LANES), lambda j, k: (layer, direction, j, 0, 0)),
        pl.BlockSpec((2, width), lambda j, k: (0, j)),
        pl.BlockSpec((1, width), lambda j, k: (0, j)),
        pl.BlockSpec((bsz, width), lambda j, k: (0, j)),
    ]
    args = [p3, p3, p3, conv_w, conv_b.reshape(1, rnn_w), w_gates, b_gates, lam.reshape(1, rnn_w), h0]
    time_major = pl.BlockSpec((tc * bsz, width), lambda j, k: (tmap(k), j))
    scratch = [pltpu.VMEM((n_lb, (tc + 2 * halo) * bsz, LANES), F32), pltpu.VMEM((bsz, width), F32)]
    if reverse:
        in_specs.append(time_major)
        args.append(hf)
        y_shape = jax.ShapeDtypeStruct((bsz, seq, rnn_w), BF16)
        y_spec = pl.BlockSpec((bsz, tc, width), lambda j, k: (0, tmap(k), j))
        scratch.append(pltpu.VMEM((n_lb, tc * bsz, LANES), F32))
    else:
        y_shape = jax.ShapeDtypeStruct((seq * bsz, rnn_w), F32)
        y_spec = time_major
    block_bytes = (2 * bsz * tc * width * 2 + 2 * bsz * tc * width * 4 * 2
                   + (tc + 2 * halo) * bsz * width * 4 + tc * bsz * width * 4)
    return pl.pallas_call(
        functools.partial(_rglru_kernel, reverse=reverse, nk=nk),
        out_shape=(y_shape, jax.ShapeDtypeStruct((bsz, rnn_w), F32)),
        grid=(rnn_w // width, nk),
        in_specs=in_specs,
        out_specs=(y_spec, pl.BlockSpec((bsz, width), lambda j, k: (0, j))),
        scratch_shapes=scratch,
        compiler_params=_cparams(("parallel", "arbitrary"), block_bytes),
        name="rglru_bwd" if reverse else "rglru_fwd",
    )(*args)


def _rglru(p3, col0, layer, conv_w, conv_b, w_gates, b_gates, lam, h0):
    hf, end_f = _rglru_dir(p3, col0, layer, conv_w, conv_b, w_gates, b_gates[0], lam[0], h0[0], None, False)
    y, end_b = _rglru_dir(p3, col0, layer, conv_w, conv_b, w_gates, b_gates[1], lam[1], h0[1], hf, True)
    return y, (end_f, end_b)


def _gelu_tanh(x):
    return 0.5 * x * (1.0 + jnp.tanh(0.7978845608028654 * (x + 0.044715 * (x * x * x))))


def _merge_kernel(x_ref, ya_ref, r_ref, gr0_ref, gr1_ref, ga0_ref, ga1_ref, gb0_ref, gb1_ref, gt_ref,
                  woa_ref, wor_ref, wout_ref, o_ref, yr_ref, acc_ref):
    c = pl.program_id(1)

    def wide(lo_ref, hi_ref):
        return jnp.concatenate([lo_ref[...], hi_ref[...]], axis=1).astype(F32)

    @pl.when(c == 0)
    def _():
        yr_ref[...] = (r_ref[...].astype(F32) * _gelu_tanh(wide(gr0_ref, gr1_ref))).astype(yr_ref.dtype)
        acc_ref[...] = jnp.zeros_like(acc_ref)

    pa = jnp.dot(ya_ref[...], woa_ref[...], preferred_element_type=F32)
    pr = jnp.dot(yr_ref[...], wor_ref[...], preferred_element_type=F32)
    t = jax.nn.sigmoid(wide(ga0_ref, ga1_ref)) * pa + jax.nn.sigmoid(wide(gb0_ref, gb1_ref)) * pr
    acc_ref[...] += jnp.dot(t.astype(BF16), wout_ref[...], preferred_element_type=F32)

    @pl.when(c == pl.num_programs(1) - 1)
    def _():
        o_ref[...] = x_ref[...] + gt_ref[...] * acc_ref[...]


def _merge(xs, ya, r, proj, gr_off, ga_off, gb_off, mods, layer, woa, wor, wout, row_of_tile):
    rows, d = xs.shape
    attn_w = ya.shape[1]
    rnn_w = r.shape[1]
    tm, tc = TM_MERGE, TC_MERGE
    half = tc // 2
    n_chunks = d // tc
    assert rnn_w == tc and gr_off % half == 0 and ga_off % half == 0 and gb_off % half == 0

    def chunk(i, c):
        return jnp.where(i % 2 == 0, c, n_chunks - 1 - c)

    def half_spec(off, k):
        return pl.BlockSpec((tm, half), lambda i, c: (i, off // half + 2 * chunk(i, c) + k))

    def gr_spec(k):
        return pl.BlockSpec((tm, half), lambda i, c: (i, gr_off // half + k))

    block_bytes = (4 * tm * d * 4 + 2 * tm * (attn_w + 2 * rnn_w + 2 * tc) * 2
                   + 2 * (attn_w + rnn_w + d) * tc * 2 + tm * rnn_w * 2 + tm * d * 4 + 3 * tm * tc * 4)
    return pl.pallas_call(
        _merge_kernel,
        out_shape=jax.ShapeDtypeStruct((rows, d), F32),
        grid=(rows // tm, d // tc),
        in_specs=[
            pl.BlockSpec((tm, d), lambda i, c: (i, 0)),
            pl.BlockSpec((tm, attn_w), lambda i, c: (i, 0)),
            pl.BlockSpec((tm, rnn_w), lambda i, c: (i, 0)),
            gr_spec(0), gr_spec(1),
            half_spec(ga_off, 0), half_spec(ga_off, 1),
            half_spec(gb_off, 0), half_spec(gb_off, 1),
            _mod_spec(layer, 5, row_of_tile, d),
            pl.BlockSpec((None, attn_w, tc), lambda i, c: (layer, 0, chunk(i, c))),
            pl.BlockSpec((None, rnn_w, tc), lambda i, c: (layer, 0, chunk(i, c))),
            pl.BlockSpec((None, tc, d), lambda i, c: (layer, chunk(i, c), 0)),
        ],
        out_specs=pl.BlockSpec((tm, d), lambda i, c: (i, 0)),
        scratch_shapes=[pltpu.VMEM((tm, rnn_w), BF16), pltpu.VMEM((tm, d), F32)],
        input_output_aliases={0: 0},
        compiler_params=_cparams(("parallel", "arbitrary"), block_bytes),
        name="merge",
    )(xs, ya, r, proj, proj, proj, proj, proj, proj, mods, woa, wor, wout)


def _up_cast_kernel(w_ref, g_ref, u_ref, *, d_ff):
    pad = g_ref.shape[1] - d_ff
    g_ref[:, :d_ff] = w_ref[:, :d_ff].astype(g_ref.dtype)
    u_ref[:, :d_ff] = w_ref[:, d_ff:].astype(u_ref.dtype)
    if pad:
        g_ref[:, d_ff:] = jnp.zeros((g_ref.shape[0], pad), g_ref.dtype)
        u_ref[:, d_ff:] = jnp.zeros((u_ref.shape[0], pad), u_ref.dtype)


def _ffn_up_weights(w_up, d_ff, ff_pad):
    depth, halves, d, n = w_up.shape
    assert n == 2 * d_ff and d_ff % LANES == 0 and d % CAST_ROWS == 0
    out = jax.ShapeDtypeStruct((depth * halves, d, ff_pad), BF16)
    wg, wu = pl.pallas_call(
        functools.partial(_up_cast_kernel, d_ff=d_ff),
        out_shape=(out, out),
        grid=(depth * halves, d // CAST_ROWS),
        in_specs=[pl.BlockSpec((None, CAST_ROWS, n), lambda l, r: (l, r, 0))],
        out_specs=(pl.BlockSpec((None, CAST_ROWS, ff_pad), lambda l, r: (l, r, 0)),
                   pl.BlockSpec((None, CAST_ROWS, ff_pad), lambda l, r: (l, r, 0))),
        compiler_params=_cparams(("parallel", "parallel"), 2 * CAST_ROWS * (n * 4 + 2 * ff_pad * 2)),
        name="ffn_up_cast",
    )(w_up.reshape(depth * halves, d, n))
    return wg.reshape(depth, halves, d, ff_pad), wu.reshape(depth, halves, d, ff_pad)


def _down_cast_kernel(w_ref, o_ref, *, tail_rows):
    r = pl.program_id(1)
    last = pl.num_programs(1) - 1

    @pl.when(r != last)
    def _():
        o_ref[...] = w_ref[...].astype(o_ref.dtype)

    @pl.when(r == last)
    def _():
        o_ref[:tail_rows, :] = w_ref[:tail_rows, :].astype(o_ref.dtype)
        if tail_rows < o_ref.shape[0]:
            o_ref[tail_rows:, :] = jnp.zeros((o_ref.shape[0] - tail_rows, o_ref.shape[1]), o_ref.dtype)


def _ffn_down_weights(w_down, ff_pad):
    depth, halves, d_ff, d = w_down.shape
    n_blocks = ff_pad // TF
    tail_rows = d_ff - (n_blocks - 1) * TF
    assert 0 < tail_rows <= TF and tail_rows % (2 * SUBLANES) == 0
    wd = pl.pallas_call(
        functools.partial(_down_cast_kernel, tail_rows=tail_rows),
        out_shape=jax.ShapeDtypeStruct((depth * halves, ff_pad, d), BF16),
        grid=(depth * halves, n_blocks),
        in_specs=[pl.BlockSpec((None, TF, d), lambda l, r: (l, r, 0))],
        out_specs=pl.BlockSpec((None, TF, d), lambda l, r: (l, r, 0)),
        compiler_params=_cparams(("parallel", "parallel"), 2 * TF * d * (4 + 2)),
        name="ffn_down_cast",
    )(w_down.reshape(depth * halves, d_ff, d))
    return wd.reshape(depth, halves, ff_pad, d)


def _rope_tables(s_len):
    pos = jnp.arange(s_len, dtype=jnp.int32)
    row = (pos // GRID_W).astype(F32)
    col = (pos % GRID_W).astype(F32)
    inv_freq = ROPE_BASE ** (-jnp.arange(ROPE_FREQS, dtype=F32) / ROPE_FREQS)
    ang_r = row[:, None] * inv_freq
    ang_c = col[:, None] * inv_freq
    cos_t = jnp.concatenate([jnp.cos(ang_r), jnp.cos(ang_r), jnp.cos(ang_c), jnp.cos(ang_c)], axis=1)
    sin_t = jnp.concatenate([-jnp.sin(ang_r), jnp.sin(ang_r), -jnp.sin(ang_c), jnp.sin(ang_c)], axis=1)
    return cos_t, sin_t


def kernel(x, c, ctx, c_ctx, w_ada, b_ada, norm_g, final_g, w_ffn_up, w_ffn_down, w_in, attn_sink,
           conv_w, conv_b, rg_w, rg_b, rg_lambda, w_o_attn, w_o_rnn, w_out):
    bsz, s_len, d = x.shape
    c_len = ctx.shape[1]
    depth = w_ada.shape[0]
    d_ff = w_ffn_down.shape[2]
    attn_w = w_o_attn.shape[1]
    rnn_w = w_o_rnn.shape[1]
    kv_w = N_KV_HEADS * HEAD_DIM
    ff_pad = -(-d_ff // TF) * TF
    n_lat, n_ctx = bsz * s_len, bsz * c_len
    assert s_len % TM_FFN == 0 and n_ctx % TM_FFN == 0 and s_len % TM_MERGE == 0 and n_ctx % TM_MERGE == 0
    assert bsz == SUBLANES and bsz + 1 <= MOD_ROWS and rnn_w % SCAN_WIDTH == 0
    assert s_len % SCAN_CHUNK == 0 and c_len % SCAN_SUB == 0 and c_len <= SCAN_CHUNK

    off_k = attn_w
    off_xr = off_k + 2 * kv_w
    off_gr = off_xr + rnn_w
    off_ga = off_gr + rnn_w
    off_gb = off_ga + d
    assert off_xr == TN_PROJ and off_k % kv_w == 0 and off_xr % SCAN_WIDTH == 0

    def lat_row(tm):
        return lambda i: i // (s_len // tm)

    def ctx_row(i):
        return bsz

    wg_all, wu_all = _ffn_up_weights(w_ffn_up, d_ff, ff_pad)
    wd_all = _ffn_down_weights(w_ffn_down, ff_pad)
    w_in_all = w_in.astype(BF16)
    woa_all, wor_all, wout_all = w_o_attn.astype(BF16), w_o_rnn.astype(BF16), w_out.astype(BF16)
    w_gates_all = (-LOG2_E * jnp.concatenate([rg_w[:, :, 0], rg_w[:, :, 1]], axis=-1)).astype(BF16)

    c_all = jnp.zeros((MOD_ROWS, d), F32).at[:bsz].set(c).at[bsz].set(c_ctx)
    mods = _adaln(c_all, w_ada, b_ada).reshape(depth, MOD_ROWS, 1, N_MOD * d)
    cos_t, sin_t = _rope_tables(s_len)
    rope = (cos_t, sin_t, attn_w + kv_w, attn_w, HEAD_DIM ** -0.5 * LOG2_E, s_len // TM_FFN)
    xl = x.reshape(n_lat, d)
    xc = ctx.reshape(n_ctx, d)
    h_zero = (jnp.zeros((bsz, rnn_w), F32), jnp.zeros((bsz, rnn_w), F32))

    for l in range(depth):
        need_ctx = l < depth - 1
        first = l == 0
        xl = _ffn(xl, mods, l, 0, norm_g[l, 0], wg_all, wu_all, wd_all, lat_row(TM_FFN), in_place=not first)
        xc = _ffn(xc, mods, l, 0, norm_g[l, 0], wg_all, wu_all, wd_all, ctx_row, in_place=not first)

        p_lat = _proj(xl, mods, l, norm_g[l, 1], w_in_all, lat_row(TM_FFN), rope=rope)
        p_ctx = _proj(xc, mods, l, norm_g[l, 1], w_in_all, ctx_row)

        sink_b = jnp.broadcast_to(attn_sink[l][:, None], (attn_sink.shape[1], LANES))
        ya_lat, ya_ctx = _attention(p_lat, p_ctx, sink_b, bsz, s_len, c_len, attn_w, off_k, need_ctx)

        r_ctx, h_end = _rglru(p_ctx.reshape(bsz, c_len, -1), off_xr, l, conv_w[l], conv_b[l], w_gates_all,
                              rg_b[l], rg_lambda[l], h_zero)
        r_lat, _ = _rglru(p_lat.reshape(bsz, s_len, -1), off_xr, l, conv_w[l], conv_b[l], w_gates_all,
                          rg_b[l], rg_lambda[l], h_end)

        xl = _merge(xl, ya_lat, r_lat.reshape(n_lat, rnn_w), p_lat, off_gr, off_ga, off_gb, mods, l,
                    woa_all, wor_all, wout_all, lat_row(TM_MERGE))
        if need_ctx:
            xc = _merge(xc, ya_ctx, r_ctx.reshape(n_ctx, rnn_w), p_ctx, off_gr, off_ga, off_gb, mods, l,
                        woa_all, wor_all, wout_all, ctx_row)
            xc = _ffn(xc, mods, l, 1, norm_g[l, 2], wg_all, wu_all, wd_all, ctx_row)
        xl = _ffn(xl, mods, l, 1, norm_g[l, 2], wg_all, wu_all, wd_all, lat_row(TM_FFN),
                  final_g=None if need_ctx else final_g)

    return xl.reshape(bsz, s_len, d)
```

```python
import functools

import jax
import jax.numpy as jnp
from jax import lax
from jax.experimental import pallas as pl
from jax.experimental.pallas import tpu as pltpu

F32 = jnp.float32
BF16 = jnp.bfloat16

LANES = 128
SUBLANES = 8
VMEM_PHYSICAL_BYTES = 64 * 1024 * 1024
VMEM_HEADROOM_BYTES = 4 * 1024 * 1024

HEAD_DIM = 128
N_KV_HEADS = 2
WINDOW_BLOCK = 128
GRID_W = 64
ROPE_BASE = 10000.0
ROPE_FREQS = HEAD_DIM // 4
CONV_LEFT = 2
RGLRU_C = 8.0
N_MOD = 9
EPS = 1e-6
MOD_ROWS = 16
NEG_BIG = -1e30
LOG2_E = 1.4426950408889634

TM_FFN = 1024
TF = 512
TN_PROJ = 1536
TM_MERGE = 512
TC_MERGE = 1024
TN_ADA = 1024
NORM_ROWS = 128
NORM_UNROLL = 4
SCAN_CHUNK = 256
SCAN_SUB = 64
SCAN_WIDTH = 512
ATTN_Q_BLOCKS = 8
CAST_ROWS = 256


def _cparams(sem, block_bytes):
    limit = min(int(block_bytes) + VMEM_HEADROOM_BYTES, VMEM_PHYSICAL_BYTES - VMEM_HEADROOM_BYTES // 2)
    return pltpu.CompilerParams(dimension_semantics=sem, vmem_limit_bytes=limit)


def _adaln_kernel(c_ref, w_ref, b_ref, o_ref):
    c = c_ref[...]
    sc = c * jax.nn.sigmoid(c)
    o_ref[...] = jnp.dot(sc, w_ref[...], preferred_element_type=F32,
                         precision=lax.Precision.HIGHEST) + b_ref[...]


def _adaln(c_all, w_ada, b_ada):
    depth, d, n = w_ada.shape
    return pl.pallas_call(
        _adaln_kernel,
        out_shape=jax.ShapeDtypeStruct((depth, MOD_ROWS, n), F32),
        grid=(depth, n // TN_ADA),
        in_specs=[
            pl.BlockSpec((MOD_ROWS, d), lambda l, j: (0, 0)),
            pl.BlockSpec((None, d, TN_ADA), lambda l, j: (l, 0, j)),
            pl.BlockSpec((None, 1, TN_ADA), lambda l, j: (l, 0, j)),
        ],
        out_specs=pl.BlockSpec((None, MOD_ROWS, TN_ADA), lambda l, j: (l, 0, j)),
        compiler_params=_cparams(("parallel", "parallel"), 2 * d * TN_ADA * 4 + 4 * MOD_ROWS * (d + TN_ADA) * 4),
        name="adaln",
    )(c_all, w_ada, b_ada.reshape(depth, 1, n))


def _rms_scale(x):
    return x * lax.rsqrt(jnp.mean(x * x, axis=-1, keepdims=True) + EPS)


def _norm_mod_store(xn_ref, rs_ref, x_ref, g_ref, sh_ref, sc_ref, zero_ref=None):
    n_steps = x_ref.shape[0] // NORM_ROWS
    inv_d = 1.0 / x_ref.shape[1]
    gs = g_ref[...] * (1.0 + sc_ref[...])
    sh = sh_ref[...]

    def stats(i, carry):
        r0 = pl.multiple_of(i * NORM_ROWS, NORM_ROWS)
        x = x_ref[pl.ds(r0, NORM_ROWS), :]
        rs_ref[pl.ds(r0, NORM_ROWS), :] = lax.rsqrt(jnp.sum(x * x, axis=-1, keepdims=True) * inv_d + EPS)
        return carry

    lax.fori_loop(0, n_steps, stats, 0, unroll=NORM_UNROLL)

    def scale(i, carry):
        r0 = pl.multiple_of(i * NORM_ROWS, NORM_ROWS)
        y = (x_ref[pl.ds(r0, NORM_ROWS), :] * rs_ref[pl.ds(r0, NORM_ROWS), :]) * gs + sh
        xn_ref[pl.ds(r0, NORM_ROWS), :] = y.astype(xn_ref.dtype)
        if zero_ref is not None:
            zero_ref[pl.ds(r0, NORM_ROWS), :] = jnp.zeros((NORM_ROWS, zero_ref.shape[1]), zero_ref.dtype)
        return carry

    lax.fori_loop(0, n_steps, scale, 0)


def _serpentine(i, c, n_chunks):
    return jnp.where(i % 2 == 0, c, n_chunks - 1 - c)


def _mod_spec(layer, k, row_of_tile, d):
    return pl.BlockSpec((None, None, 1, d), lambda i, j: (layer, row_of_tile(i), 0, k))


def _ffn_kernel(x_ref, g_ref, sh_ref, sc_ref, gt_ref, wg_ref, wu_ref, wd_ref, *refs, final_norm):
    if final_norm:
        fg_ref, o_ref, xn_ref, rs_ref = refs
    else:
        o_ref, xn_ref, rs_ref = refs
    c = pl.program_id(1)

    @pl.when(c == 0)
    def _():
        _norm_mod_store(xn_ref, rs_ref, x_ref, g_ref, sh_ref, sc_ref, zero_ref=o_ref)

    xn = xn_ref[...]
    h = jnp.dot(xn, wg_ref[...], preferred_element_type=F32)
    u = jnp.dot(xn, wu_ref[...], preferred_element_type=F32)
    a = ((h * jax.nn.sigmoid(h)) * u).astype(BF16)
    o_ref[...] += jnp.dot(a, wd_ref[...], preferred_element_type=F32)

    @pl.when(c == pl.num_programs(1) - 1)
    def _():
        gate = 0.5 * gt_ref[...]

        def body(i, carry):
            r0 = pl.multiple_of(i * NORM_ROWS, NORM_ROWS)
            y = x_ref[pl.ds(r0, NORM_ROWS), :] + gate * o_ref[pl.ds(r0, NORM_ROWS), :]
            if final_norm:
                y = _rms_scale(y) * fg_ref[...]
            o_ref[pl.ds(r0, NORM_ROWS), :] = y
            return carry

        lax.fori_loop(0, x_ref.shape[0] // NORM_ROWS, body, 0)


def _ffn(xs, mods, layer, half, g, wg, wu, wd, row_of_tile, final_g=None, in_place=True):
    rows, d = xs.shape
    ff = wg.shape[-1]
    tm = TM_FFN
    mod_k0 = 6 * half
    final_norm = final_g is not None
    in_specs = [
        pl.BlockSpec((tm, d), lambda i, c: (i, 0)),
        pl.BlockSpec((1, d), lambda i, c: (0, 0)),
        _mod_spec(layer, mod_k0, row_of_tile, d),
        _mod_spec(layer, mod_k0 + 1, row_of_tile, d),
        _mod_spec(layer, mod_k0 + 2, row_of_tile, d),
        pl.BlockSpec((None, None, d, TF), lambda i, c: (layer, half, 0, _serpentine(i, c, ff // TF))),
        pl.BlockSpec((None, None, d, TF), lambda i, c: (layer, half, 0, _serpentine(i, c, ff // TF))),
        pl.BlockSpec((None, None, TF, d), lambda i, c: (layer, half, _serpentine(i, c, ff // TF), 0)),
    ]
    args = [xs, g.reshape(1, d), mods, mods, mods, wg, wu, wd]
    if final_norm:
        in_specs.append(pl.BlockSpec((1, d), lambda i, c: (0, 0)))
        args.append(final_g.reshape(1, d))
    block_bytes = (2 * tm * d * 4 + 2 * tm * d * 4 + 2 * 3 * d * TF * 2 + tm * d * 2
                   + 2 * tm * TF * 4 + tm * TF * 2)
    return pl.pallas_call(
        functools.partial(_ffn_kernel, final_norm=final_norm),
        out_shape=jax.ShapeDtypeStruct((rows, d), F32),
        grid=(rows // tm, ff // TF),
        in_specs=in_specs,
        out_specs=pl.BlockSpec((tm, d), lambda i, c: (i, 0)),
        scratch_shapes=[pltpu.VMEM((tm, d), BF16), pltpu.VMEM((tm, 1), F32)],
        input_output_aliases={0: 0} if (in_place and not final_norm) else {},
        compiler_params=_cparams(("parallel", "arbitrary"), block_bytes),
        name="ffn_final" if final_norm else "ffn",
    )(*args)


def _rope(t, cs, sn):
    lane = lax.broadcasted_iota(jnp.int32, t.shape, 1)
    first_half = (lane & (2 * ROPE_FREQS - 1)) < ROPE_FREQS
    partner = jnp.where(first_half,
                        pltpu.roll(t, HEAD_DIM - ROPE_FREQS, axis=1),
                        pltpu.roll(t, ROPE_FREQS, axis=1))
    return t * cs + partner * sn


def _proj_kernel(x_ref, g_ref, sh_ref, sc_ref, w_ref, *refs, rope_cols, q_cols, q_scale):
    if rope_cols:
        cos_ref, sin_ref, o_ref, xn_ref, rs_ref = refs
    else:
        o_ref, xn_ref, rs_ref = refs
    j = pl.program_id(1)

    @pl.when(j == 0)
    def _():
        _norm_mod_store(xn_ref, rs_ref, x_ref, g_ref, sh_ref, sc_ref)

    def plain():
        o_ref[...] = jnp.dot(xn_ref[...], w_ref[...], preferred_element_type=F32).astype(o_ref.dtype)

    if not rope_cols:
        plain()
        return

    col_tile = _serpentine(pl.program_id(0), j, pl.num_programs(1))

    @pl.when(col_tile == 0)
    def _():
        xn = xn_ref[...]
        cs = cos_ref[...]
        sn = sin_ref[...]
        pair = 2 * HEAD_DIM
        for c0 in range(0, o_ref.shape[1], pair):
            res = jnp.dot(xn, w_ref[:, c0:c0 + pair], preferred_element_type=F32)
            for h0 in range(0, pair, HEAD_DIM):
                t = res[:, h0:h0 + HEAD_DIM]
                if c0 + h0 < rope_cols:
                    t = _rope(t, cs, sn)
                if c0 + h0 < q_cols:
                    t = t * q_scale
                o_ref[:, c0 + h0:c0 + h0 + HEAD_DIM] = t.astype(o_ref.dtype)

    pl.when(col_tile != 0)(plain)


def _proj(xs, mods, layer, g, w, row_of_tile, rope=None):
    rows, d = xs.shape
    n = w.shape[-1]
    tm = TM_FFN
    in_specs = [
        pl.BlockSpec((tm, d), lambda i, j: (i, 0)),
        pl.BlockSpec((1, d), lambda i, j: (0, 0)),
        _mod_spec(layer, 3, row_of_tile, d),
        _mod_spec(layer, 4, row_of_tile, d),
        pl.BlockSpec((None, d, TN_PROJ), lambda i, j: (layer, 0, _serpentine(i, j, n // TN_PROJ))),
    ]
    args = [xs, g.reshape(1, d), mods, mods, w]
    rope_cols = q_cols = 0
    q_scale = 1.0
    if rope is not None:
        cos_t, sin_t, rope_cols, q_cols, q_scale, tiles_per_sample = rope
        assert q_cols <= rope_cols <= TN_PROJ
        in_specs += [pl.BlockSpec((tm, HEAD_DIM), lambda i, j: (i % tiles_per_sample, 0))] * 2
        args += [cos_t, sin_t]
    block_bytes = (2 * tm * d * 4 + tm * d * 2 + 2 * d * TN_PROJ * 2 + 2 * tm * TN_PROJ * 2
                   + tm * TN_PROJ * 4 + 4 * tm * HEAD_DIM * 4)
    return pl.pallas_call(
        functools.partial(_proj_kernel, rope_cols=rope_cols, q_cols=q_cols, q_scale=q_scale),
        out_shape=jax.ShapeDtypeStruct((rows, n), BF16),
        grid=(rows // tm, n // TN_PROJ),
        in_specs=in_specs,
        out_specs=pl.BlockSpec((tm, TN_PROJ), lambda i, j: (i, _serpentine(i, j, n // TN_PROJ))),
        scratch_shapes=[pltpu.VMEM((tm, d), BF16), pltpu.VMEM((tm, 1), F32)],
        compiler_params=_cparams(("parallel", "arbitrary"), block_bytes),
        name="in_proj_rope" if rope_cols else "in_proj",
    )(*args)


def _scores(q4, kcat):
    return lax.dot_general(q4, kcat, (((1,), (1,)), ((), ())), preferred_element_type=F32)


def _softmax_pv(s, vcat, sk, bias, scale):
    if scale is None:
        sk = sk * LOG2_E
        exp = jnp.exp2
    else:
        s = s * scale
        exp = jnp.exp
    if bias is not None:
        rb = bias.shape[0]
        s = jnp.concatenate([s[r0:r0 + rb] + bias for r0 in range(0, s.shape[0], rb)], axis=0)
    m = jnp.maximum(jnp.max(s, axis=-1, keepdims=True), sk)
    p = exp(s - m)
    denom = jnp.sum(p, axis=-1, keepdims=True) + exp(sk - m)
    o = jnp.dot(p.astype(BF16), vcat, preferred_element_type=F32)
    return o / denom


def _sink_rows(sink_ref, g, q_per_kv, rows):
    parts = [jnp.broadcast_to(sink_ref[g * q_per_kv + j:g * q_per_kv + j + 1, 0:1], (rows, 1))
             for j in range(q_per_kv)]
    return jnp.concatenate(parts, axis=0)


def _attn_lat_kernel(q_ref, kp_ref, kc_ref, kn_ref, vp_ref, vc_ref, vn_ref, kx_ref, vx_ref,
                     sink_ref, o_ref, *, nb, q_per_kv, scale):
    n = pl.program_id(1)
    blk = WINDOW_BLOCK
    dh = HEAD_DIM
    c_len = kx_ref.shape[0]
    nq = q_ref.shape[0] // blk
    keys = 3 * blk + c_len
    ri = lax.broadcasted_iota(jnp.int32, (blk, keys), 0)
    kj = lax.broadcasted_iota(jnp.int32, (blk, keys), 1)
    band = (kj >= 3 * blk) | ((kj >= ri) & (kj <= ri + 2 * blk))

    def window(qb, p_ref, c_ref, n_ref, x_ref, cols):
        parts = []
        for w in (qb - 1, qb, qb + 1):
            if w < 0:
                parts.append(p_ref[:, cols])
            elif w >= nq:
                parts.append(n_ref[:, cols])
            else:
                parts.append(c_ref[w * blk:(w + 1) * blk, cols])
        return jnp.concatenate(parts + [x_ref[:, cols]], axis=0)

    def scores(qb, g):
        q4 = jnp.concatenate([q_ref[qb * blk:(qb + 1) * blk, (g * q_per_kv + j) * dh:(g * q_per_kv + j + 1) * dh]
                              for j in range(q_per_kv)], axis=0)
        return _scores(q4, window(qb, kp_ref, kc_ref, kn_ref, kx_ref, slice(g * dh, (g + 1) * dh)))

    streams = [(qb, g) for qb in range(nq) for g in range(N_KV_HEADS)]
    s_next = scores(*streams[0])
    for i, (qb, g) in enumerate(streams):
        s_cur = s_next
        if i + 1 < len(streams):
            s_next = scores(*streams[i + 1])
        first = (n == 0) if qb == 0 else False
        last = (n == nb // nq - 1) if qb == nq - 1 else False
        lo = jnp.where(first, blk, 0)
        hi = jnp.where(last, 2 * blk, 3 * blk)
        bias = jnp.where(band & ((kj >= 3 * blk) | ((kj >= lo) & (kj < hi))), 0.0, NEG_BIG).astype(F32)
        vcat = window(qb, vp_ref, vc_ref, vn_ref, vx_ref, slice(g * dh, (g + 1) * dh))
        o = _softmax_pv(s_cur, vcat, _sink_rows(sink_ref, g, q_per_kv, blk), bias, scale)
        for j in range(q_per_kv):
            h = g * q_per_kv + j
            o_ref[qb * blk:(qb + 1) * blk, h * dh:(h + 1) * dh] = o[j * blk:(j + 1) * blk].astype(o_ref.dtype)


def _attn_ctx_kernel(q_ref, kx_ref, vx_ref, sink_ref, o_ref, *, q_per_kv, scale):
    dh = HEAD_DIM
    c_len = q_ref.shape[0]
    for g in range(N_KV_HEADS):
        cols = slice(g * dh, (g + 1) * dh)
        q4 = jnp.concatenate([q_ref[:, (g * q_per_kv + j) * dh:(g * q_per_kv + j + 1) * dh]
                              for j in range(q_per_kv)], axis=0)
        sk = _sink_rows(sink_ref, g, q_per_kv, c_len)
        o = _softmax_pv(_scores(q4, kx_ref[:, cols]), vx_ref[:, cols], sk, None, scale)
        for j in range(q_per_kv):
            h = g * q_per_kv + j
            o_ref[:, h * dh:(h + 1) * dh] = o[j * c_len:(j + 1) * c_len].astype(o_ref.dtype)


def _attention(p_lat, p_ctx, sink_b, bsz, s_len, c_len, attn_w, k_off, need_ctx):
    kv_w = N_KV_HEADS * HEAD_DIM
    n_q_heads = attn_w // HEAD_DIM
    q_per_kv = n_q_heads // N_KV_HEADS
    blk = WINDOW_BLOCK
    nb = s_len // blk
    k_col = k_off // kv_w
    v_col = k_col + 1
    scale = HEAD_DIM ** -0.5

    nq = ATTN_Q_BLOCKS
    steps = nb // nq
    assert nb % nq == 0

    def attn_bytes(q_rows):
        blocks = 2 * 2 * (2 * q_rows * attn_w + 2 * (q_rows + 2 * blk + c_len) * kv_w)
        return blocks + 8 * q_per_kv * max(blk, c_len) * (3 * blk + c_len) * 4

    def kv_specs(col):
        prev = pl.BlockSpec((blk, kv_w), lambda b, n: (b * nb + jnp.maximum(n * nq - 1, 0), col))
        own = pl.BlockSpec((nq * blk, kv_w), lambda b, n: (b * steps + n, col))
        nxt = pl.BlockSpec((blk, kv_w), lambda b, n: (b * nb + jnp.minimum((n + 1) * nq, nb - 1), col))
        return [prev, own, nxt]

    ya_lat = pl.pallas_call(
        functools.partial(_attn_lat_kernel, nb=nb, q_per_kv=q_per_kv, scale=None),
        out_shape=jax.ShapeDtypeStruct((bsz * s_len, attn_w), BF16),
        grid=(bsz, steps),
        in_specs=[pl.BlockSpec((nq * blk, attn_w), lambda b, n: (b * steps + n, 0))]
        + kv_specs(k_col) + kv_specs(v_col) + [
            pl.BlockSpec((c_len, kv_w), lambda b, n: (b, k_col)),
            pl.BlockSpec((c_len, kv_w), lambda b, n: (b, v_col)),
            pl.BlockSpec((n_q_heads, LANES), lambda b, n: (0, 0)),
        ],
        out_specs=pl.BlockSpec((nq * blk, attn_w), lambda b, n: (b * steps + n, 0)),
        compiler_params=_cparams(("parallel", "parallel"), attn_bytes(nq * blk)),
        name="attn_latent",
    )(p_lat, p_lat, p_lat, p_lat, p_lat, p_lat, p_lat, p_ctx, p_ctx, sink_b)
    if not need_ctx:
        return ya_lat, None
    ya_ctx = pl.pallas_call(
        functools.partial(_attn_ctx_kernel, q_per_kv=q_per_kv, scale=scale),
        out_shape=jax.ShapeDtypeStruct((bsz * c_len, attn_w), BF16),
        grid=(bsz,),
        in_specs=[
            pl.BlockSpec((c_len, attn_w), lambda b: (b, 0)),
            pl.BlockSpec((c_len, kv_w), lambda b: (b, k_col)),
            pl.BlockSpec((c_len, kv_w), lambda b: (b, v_col)),
            pl.BlockSpec((n_q_heads, LANES), lambda b: (0, 0)),
        ],
        out_specs=pl.BlockSpec((c_len, attn_w), lambda b: (b, 0)),
        compiler_params=_cparams(("parallel",), attn_bytes(c_len)),
        name="attn_context",
    )(p_ctx, p_ctx, p_ctx, sink_b)
    return ya_lat, ya_ctx


def _rglru_kernel(xp_ref, x_ref, xn_ref, cw_ref, cb_ref, w_ref, b_ref, lam_ref, h0_ref, *refs,
                  reverse, nk):
    if reverse:
        hf_ref, y_ref, hend_ref, xt_ref, h_ref, hs_ref = refs
    else:
        y_ref, hend_ref, xt_ref, h_ref = refs
    k = pl.program_id(1)
    ci = nk - 1 - k if reverse else k
    bsz, tc, width = x_ref.shape
    n_lb = width // LANES
    halo = xp_ref.shape[1]
    n_sub = tc // SCAN_SUB
    rows_sub = SCAN_SUB * bsz

    @pl.when(k == 0)
    def _():
        h_ref[...] = h0_ref[...]

    for c in range(n_lb):
        lanes = slice(c * LANES, (c + 1) * LANES)
        for b in range(bsz):
            xt_ref[c, pl.ds(b, halo, stride=bsz), :] = xp_ref[b, :, lanes].astype(F32)
            xt_ref[c, pl.ds(halo * bsz + b, tc, stride=bsz), :] = x_ref[b, :, lanes].astype(F32)
            xt_ref[c, pl.ds((halo + tc) * bsz + b, halo, stride=bsz), :] = xn_ref[b, :, lanes].astype(F32)

    @pl.when(ci == 0)
    def _():
        xt_ref[:, 0:halo * bsz, :] = jnp.zeros((n_lb, halo * bsz, LANES), F32)

    @pl.when(ci == nk - 1)
    def _():
        xt_ref[:, (halo + tc) * bsz:(2 * halo + tc) * bsz, :] = jnp.zeros((n_lb, halo * bsz, LANES), F32)

    def sub_body(i, hs):
        si = (n_sub - 1 - i) if reverse else i
        t0 = si * SCAN_SUB
        new_h = []
        for c in range(n_lb):
            lanes = slice(c * LANES, (c + 1) * LANES)
            cw = cw_ref[:, lanes]
            u = cb_ref[:, lanes]
            for kk in range(cw.shape[0]):
                r0 = pl.multiple_of((t0 + halo - CONV_LEFT + kk) * bsz, bsz)
                u = u + cw[kk:kk + 1, :] * xt_ref[c, pl.ds(r0, rows_sub), :]
            g = jnp.dot(u.astype(BF16), w_ref[c], preferred_element_type=F32)
            gr = 1.0 / (1.0 + jnp.exp2(g[:, :LANES] - LOG2_E * b_ref[0:1, lanes]))
            gi = 1.0 / (1.0 + jnp.exp2(g[:, LANES:] - LOG2_E * b_ref[1:2, lanes]))
            neg = -lam_ref[:, lanes]
            softplus = jnp.maximum(neg, 0.0) + jnp.log1p(jnp.exp(-jnp.abs(neg)))
            log_a = gr * ((-RGLRU_C) * softplus)
            a = jnp.exp(log_a)
            x1 = -jnp.tanh(log_a) * (a * a + 1.0)
            m = jnp.where(x1 > 0.0, x1 * lax.rsqrt(x1), 0.0)
            bx = m * (gi * u)
            h = hs[c]
            outs = [None] * SCAN_SUB
            for t in (range(SCAN_SUB - 1, -1, -1) if reverse else range(SCAN_SUB)):
                h = a[t * bsz:(t + 1) * bsz] * h + bx[t * bsz:(t + 1) * bsz]
                outs[t] = h
            rows = pl.ds(pl.multiple_of(t0 * bsz, rows_sub), rows_sub)
            if reverse:
                hs_ref[c, rows, :] = jnp.concatenate(outs, axis=0) + hf_ref[rows, lanes]
            else:
                y_ref[rows, lanes] = jnp.concatenate(outs, axis=0)
            new_h.append(h)
        return tuple(new_h)

    h_init = tuple(h_ref[:, c * LANES:(c + 1) * LANES] for c in range(n_lb))
    h_fin = lax.fori_loop(0, n_sub, sub_body, h_init)
    for c in range(n_lb):
        lanes = slice(c * LANES, (c + 1) * LANES)
        h_ref[:, lanes] = h_fin[c]
        if reverse:
            for b in range(bsz):
                y_ref[b, :, lanes] = hs_ref[c, pl.ds(b, tc, stride=bsz), :].astype(y_ref.dtype)

    @pl.when(k == nk - 1)
    def _():
        hend_ref[...] = h_ref[...]


def _rglru_dir(p3, col0, layer, conv_w, conv_b, w_gates, b_gates, lam, h0, hf, reverse):
    direction = 1 if reverse else 0
    bsz, seq, _ = p3.shape
    rnn_w = conv_w.shape[-1]
    width = SCAN_WIDTH
    tc = min(SCAN_CHUNK, seq)
    nk = seq // tc
    n_lb = width // LANES
    cblk0 = col0 // width
    halo = 2 * SUBLANES
    hpt = tc // halo
    n_halo_blocks = seq // halo

    def tmap(k):
        return nk - 1 - k if reverse else k

    in_specs = [
        pl.BlockSpec((bsz, halo, width), lambda j, k: (0, jnp.maximum(tmap(k) * hpt - 1, 0), cblk0 + j)),
        pl.BlockSpec((bsz, tc, width), lambda j, k: (0, tmap(k), cblk0 + j)),
        pl.BlockSpec((bsz, halo, width),
                     lambda j, k: (0, jnp.minimum((tmap(k) + 1) * hpt, n_halo_blocks - 1), cblk0 + j)),
        pl.BlockSpec((conv_w.shape[0], width), lambda j, k: (0, j)),
        pl.BlockSpec((1, width), lambda j, k: (0, j)),
        pl.BlockSpec((None, None, n_lb, LANES, 2 * LANES), lambda j, k: (layer, direction, j, 0, 0)),
        pl.BlockSpec((2, width), lambda j, k: (0, j)),
        pl.BlockSpec((1, width), lambda j, k: (0, j)),
        pl.BlockSpec((bsz, width), lambda j, k: (0, j)),
    ]
    args = [p3, p3, p3, conv_w, conv_b.reshape(1, rnn_w), w_gates, b_gates, lam.reshape(1, rnn_w), h0]
    time_major = pl.BlockSpec((tc * bsz, width), lambda j, k: (tmap(k), j))
    scratch = [pltpu.VMEM((n_lb, (tc + 2 * halo) * bsz, LANES), F32), pltpu.VMEM((bsz, width), F32)]
    if reverse:
        in_specs.append(time_major)
        args.append(hf)
        y_shape = jax.ShapeDtypeStruct((bsz, seq, rnn_w), BF16)
        y_spec = pl.BlockSpec((bsz, tc, width), lambda j, k: (0, tmap(k), j))
        scratch.append(pltpu.VMEM((n_lb, tc * bsz, LANES), F32))
    else:
        y_shape = jax.ShapeDtypeStruct((seq * bsz, rnn_w), F32)
        y_spec = time_major
    block_bytes = (2 * bsz * tc * width * 2 + 2 * bsz * tc * width * 4 * 2
                   + (tc + 2 * halo) * bsz * width * 4 + tc * bsz * width * 4)
    return pl.pallas_call(
        functools.partial(_rglru_kernel, reverse=reverse, nk=nk),
        out_shape=(y_shape, jax.ShapeDtypeStruct((bsz, rnn_w), F32)),
        grid=(rnn_w // width, nk),
        in_specs=in_specs,
        out_specs=(y_spec, pl.BlockSpec((bsz, width), lambda j, k: (0, j))),
        scratch_shapes=scratch,
        compiler_params=_cparams(("parallel", "arbitrary"), block_bytes),
        name="rglru_bwd" if reverse else "rglru_fwd",
    )(*args)


def _rglru(p3, col0, layer, conv_w, conv_b, w_gates, b_gates, lam, h0):
    hf, end_f = _rglru_dir(p3, col0, layer, conv_w, conv_b, w_gates, b_gates[0], lam[0], h0[0], None, False)
    y, end_b = _rglru_dir(p3, col0, layer, conv_w, conv_b, w_gates, b_gates[1], lam[1], h0[1], hf, True)
    return y, (end_f, end_b)


def _gelu_tanh(x):
    return 0.5 * x * (1.0 + jnp.tanh(0.7978845608028654 * (x + 0.044715 * (x * x * x))))


def _merge_kernel(x_ref, ya_ref, r_ref, gr0_ref, gr1_ref, ga0_ref, ga1_ref, gb0_ref, gb1_ref, gt_ref,
                  woa_ref, wor_ref, wout_ref, o_ref, yr_ref, acc_ref):
    c = pl.program_id(1)

    def wide(lo_ref, hi_ref):
        return jnp.concatenate([lo_ref[...], hi_ref[...]], axis=1).astype(F32)

    @pl.when(c == 0)
    def _():
        yr_ref[...] = (r_ref[...].astype(F32) * _gelu_tanh(wide(gr0_ref, gr1_ref))).astype(yr_ref.dtype)
        acc_ref[...] = jnp.zeros_like(acc_ref)

    pa = jnp.dot(ya_ref[...], woa_ref[...], preferred_element_type=F32)
    pr = jnp.dot(yr_ref[...], wor_ref[...], preferred_element_type=F32)
    t = jax.nn.sigmoid(wide(ga0_ref, ga1_ref)) * pa + jax.nn.sigmoid(wide(gb0_ref, gb1_ref)) * pr
    acc_ref[...] += jnp.dot(t.astype(BF16), wout_ref[...], preferred_element_type=F32)

    @pl.when(c == pl.num_programs(1) - 1)
    def _():
        o_ref[...] = x_ref[...] + gt_ref[...] * acc_ref[...]


def _merge(xs, ya, r, proj, gr_off, ga_off, gb_off, mods, layer, woa, wor, wout, row_of_tile):
    rows, d = xs.shape
    attn_w = ya.shape[1]
    rnn_w = r.shape[1]
    tm, tc = TM_MERGE, TC_MERGE
    half = tc // 2
    n_chunks = d // tc
    assert rnn_w == tc and gr_off % half == 0 and ga_off % half == 0 and gb_off % half == 0

    def chunk(i, c):
        return _serpentine(i, c, n_chunks)

    def half_spec(off, k):
        return pl.BlockSpec((tm, half), lambda i, c: (i, off // half + 2 * chunk(i, c) + k))

    def gr_spec(k):
        return pl.BlockSpec((tm, half), lambda i, c: (i, gr_off // half + k))

    block_bytes = (4 * tm * d * 4 + 2 * tm * (attn_w + 2 * rnn_w + 2 * tc) * 2
                   + 2 * (attn_w + rnn_w + d) * tc * 2 + tm * rnn_w * 2 + tm * d * 4 + 3 * tm * tc * 4)
    return pl.pallas_call(
        _merge_kernel,
        out_shape=jax.ShapeDtypeStruct((rows, d), F32),
        grid=(rows // tm, d // tc),
        in_specs=[
            pl.BlockSpec((tm, d), lambda i, c: (i, 0)),
            pl.BlockSpec((tm, attn_w), lambda i, c: (i, 0)),
            pl.BlockSpec((tm, rnn_w), lambda i, c: (i, 0)),
            gr_spec(0), gr_spec(1),
            half_spec(ga_off, 0), half_spec(ga_off, 1),
            half_spec(gb_off, 0), half_spec(gb_off, 1),
            _mod_spec(layer, 5, row_of_tile, d),
            pl.BlockSpec((None, attn_w, tc), lambda i, c: (layer, 0, chunk(i, c))),
            pl.BlockSpec((None, rnn_w, tc), lambda i, c: (layer, 0, chunk(i, c))),
            pl.BlockSpec((None, tc, d), lambda i, c: (layer, chunk(i, c), 0)),
        ],
        out_specs=pl.BlockSpec((tm, d), lambda i, c: (i, 0)),
        scratch_shapes=[pltpu.VMEM((tm, rnn_w), BF16), pltpu.VMEM((tm, d), F32)],
        input_output_aliases={0: 0},
        compiler_params=_cparams(("parallel", "arbitrary"), block_bytes),
        name="merge",
    )(xs, ya, r, proj, proj, proj, proj, proj, proj, mods, woa, wor, wout)


def _up_cast_kernel(w_ref, g_ref, u_ref, *, d_ff):
    pad = g_ref.shape[1] - d_ff
    g_ref[:, :d_ff] = w_ref[:, :d_ff].astype(g_ref.dtype)
    u_ref[:, :d_ff] = w_ref[:, d_ff:].astype(u_ref.dtype)
    if pad:
        g_ref[:, d_ff:] = jnp.zeros((g_ref.shape[0], pad), g_ref.dtype)
        u_ref[:, d_ff:] = jnp.zeros((u_ref.shape[0], pad), u_ref.dtype)


def _ffn_up_weights(w_up, d_ff, ff_pad):
    depth, halves, d, n = w_up.shape
    assert n == 2 * d_ff and d_ff % LANES == 0 and d % CAST_ROWS == 0
    out = jax.ShapeDtypeStruct((depth * halves, d, ff_pad), BF16)
    wg, wu = pl.pallas_call(
        functools.partial(_up_cast_kernel, d_ff=d_ff),
        out_shape=(out, out),
        grid=(depth * halves, d // CAST_ROWS),
        in_specs=[pl.BlockSpec((None, CAST_ROWS, n), lambda l, r: (l, r, 0))],
        out_specs=(pl.BlockSpec((None, CAST_ROWS, ff_pad), lambda l, r: (l, r, 0)),
                   pl.BlockSpec((None, CAST_ROWS, ff_pad), lambda l, r: (l, r, 0))),
        compiler_params=_cparams(("parallel", "parallel"), 2 * CAST_ROWS * (n * 4 + 2 * ff_pad * 2)),
        name="ffn_up_cast",
    )(w_up.reshape(depth * halves, d, n))
    return wg.reshape(depth, halves, d, ff_pad), wu.reshape(depth, halves, d, ff_pad)


def _down_cast_kernel(w_ref, o_ref, *, tail_rows):
    r = pl.program_id(1)
    last = pl.num_programs(1) - 1

    @pl.when(r != last)
    def _():
        o_ref[...] = w_ref[...].astype(o_ref.dtype)

    @pl.when(r == last)
    def _():
        o_ref[:tail_rows, :] = w_ref[:tail_rows, :].astype(o_ref.dtype)
        if tail_rows < o_ref.shape[0]:
            o_ref[tail_rows:, :] = jnp.zeros((o_ref.shape[0] - tail_rows, o_ref.shape[1]), o_ref.dtype)


def _ffn_down_weights(w_down, ff_pad):
    depth, halves, d_ff, d = w_down.shape
    n_blocks = ff_pad // TF
    tail_rows = d_ff - (n_blocks - 1) * TF
    assert 0 < tail_rows <= TF and tail_rows % (2 * SUBLANES) == 0
    wd = pl.pallas_call(
        functools.partial(_down_cast_kernel, tail_rows=tail_rows),
        out_shape=jax.ShapeDtypeStruct((depth * halves, ff_pad, d), BF16),
        grid=(depth * halves, n_blocks),
        in_specs=[pl.BlockSpec((None, TF, d), lambda l, r: (l, r, 0))],
        out_specs=pl.BlockSpec((None, TF, d), lambda l, r: (l, r, 0)),
        compiler_params=_cparams(("parallel", "parallel"), 2 * TF * d * (4 + 2)),
        name="ffn_down_cast",
    )(w_down.reshape(depth * halves, d_ff, d))
    return wd.reshape(depth, halves, ff_pad, d)


def _rope_tables(s_len):
    pos = jnp.arange(s_len, dtype=jnp.int32)
    row = (pos // GRID_W).astype(F32)
    col = (pos % GRID_W).astype(F32)
    inv_freq = ROPE_BASE ** (-jnp.arange(ROPE_FREQS, dtype=F32) / ROPE_FREQS)
    ang_r = row[:, None] * inv_freq
    ang_c = col[:, None] * inv_freq
    cos_t = jnp.concatenate([jnp.cos(ang_r), jnp.cos(ang_r), jnp.cos(ang_c), jnp.cos(ang_c)], axis=1)
    sin_t = jnp.concatenate([-jnp.sin(ang_r), jnp.sin(ang_r), -jnp.sin(ang_c), jnp.sin(ang_c)], axis=1)
    return cos_t, sin_t


def kernel(x, c, ctx, c_ctx, w_ada, b_ada, norm_g, final_g, w_ffn_up, w_ffn_down, w_in, attn_sink,
           conv_w, conv_b, rg_w, rg_b, rg_lambda, w_o_attn, w_o_rnn, w_out):
    bsz, s_len, d = x.shape
    c_len = ctx.shape[1]
    depth = w_ada.shape[0]
    d_ff = w_ffn_down.shape[2]
    attn_w = w_o_attn.shape[1]
    rnn_w = w_o_rnn.shape[1]
    kv_w = N_KV_HEADS * HEAD_DIM
    ff_pad = -(-d_ff // TF) * TF
    n_lat, n_ctx = bsz * s_len, bsz * c_len
    assert s_len % TM_FFN == 0 and n_ctx % TM_FFN == 0 and s_len % TM_MERGE == 0 and n_ctx % TM_MERGE == 0
    assert bsz == SUBLANES and bsz + 1 <= MOD_ROWS and rnn_w % SCAN_WIDTH == 0
    assert s_len % SCAN_CHUNK == 0 and c_len % SCAN_SUB == 0 and c_len <= SCAN_CHUNK

    off_k = attn_w
    off_xr = off_k + 2 * kv_w
    off_gr = off_xr + rnn_w
    off_ga = off_gr + rnn_w
    off_gb = off_ga + d
    assert off_xr == TN_PROJ and off_k % kv_w == 0 and off_xr % SCAN_WIDTH == 0

    def lat_row(tm):
        return lambda i: i // (s_len // tm)

    def ctx_row(i):
        return bsz

    wg_all, wu_all = _ffn_up_weights(w_ffn_up, d_ff, ff_pad)
    wd_all = _ffn_down_weights(w_ffn_down, ff_pad)
    w_in_all = w_in.astype(BF16)
    woa_all, wor_all, wout_all = w_o_attn.astype(BF16), w_o_rnn.astype(BF16), w_out.astype(BF16)
    w_gates_all = (-LOG2_E * jnp.concatenate([rg_w[:, :, 0], rg_w[:, :, 1]], axis=-1)).astype(BF16)

    c_all = jnp.zeros((MOD_ROWS, d), F32).at[:bsz].set(c).at[bsz].set(c_ctx)
    mods = _adaln(c_all, w_ada, b_ada).reshape(depth, MOD_ROWS, 1, N_MOD * d)
    cos_t, sin_t = _rope_tables(s_len)
    rope = (cos_t, sin_t, attn_w + kv_w, attn_w, HEAD_DIM ** -0.5 * LOG2_E, s_len // TM_FFN)
    xl = x.reshape(n_lat, d)
    xc = ctx.reshape(n_ctx, d)
    h_zero = (jnp.zeros((bsz, rnn_w), F32), jnp.zeros((bsz, rnn_w), F32))

    for l in range(depth):
        need_ctx = l < depth - 1
        first = l == 0
        xl = _ffn(xl, mods, l, 0, norm_g[l, 0], wg_all, wu_all, wd_all, lat_row(TM_FFN), in_place=not first)
        xc = _ffn(xc, mods, l, 0, norm_g[l, 0], wg_all, wu_all, wd_all, ctx_row, in_place=not first)

        p_lat = _proj(xl, mods, l, norm_g[l, 1], w_in_all, lat_row(TM_FFN), rope=rope)
        p_ctx = _proj(xc, mods, l, norm_g[l, 1], w_in_all, ctx_row)

        sink_b = jnp.broadcast_to(attn_sink[l][:, None], (attn_sink.shape[1], LANES))
        ya_lat, ya_ctx = _attention(p_lat, p_ctx, sink_b, bsz, s_len, c_len, attn_w, off_k, need_ctx)

        r_ctx, h_end = _rglru(p_ctx.reshape(bsz, c_len, -1), off_xr, l, conv_w[l], conv_b[l], w_gates_all,
                              rg_b[l], rg_lambda[l], h_zero)
        r_lat, _ = _rglru(p_lat.reshape(bsz, s_len, -1), off_xr, l, conv_w[l], conv_b[l], w_gates_all,
                          rg_b[l], rg_lambda[l], h_end)

        xl = _merge(xl, ya_lat, r_lat.reshape(n_lat, rnn_w), p_lat, off_gr, off_ga, off_gb, mods, l,
                    woa_all, wor_all, wout_all, lat_row(TM_MERGE))
        if need_ctx:
            xc = _merge(xc, ya_ctx, r_ctx.reshape(n_ctx, rnn_w), p_ctx, off_gr, off_ga, off_gb, mods, l,
                        woa_all, wor_all, wout_all, ctx_row)
            xc = _ffn(xc, mods, l, 1, norm_g[l, 2], wg_all, wu_all, wd_all, ctx_row)
        xl = _ffn(xl, mods, l, 1, norm_g[l, 2], wg_all, wu_all, wd_all, lat_row(TM_FFN),
                  final_g=None if need_ctx else final_g)

    return xl.reshape(bsz, s_len, d)
```

```python
import functools

import jax
import jax.numpy as jnp
from jax import lax
from jax.experimental import pallas as pl
from jax.experimental.pallas import tpu as pltpu

F32 = jnp.float32
BF16 = jnp.bfloat16

LANES = 128
SUBLANES = 8
VMEM_PHYSICAL_BYTES = 64 * 1024 * 1024
VMEM_HEADROOM_BYTES = 4 * 1024 * 1024

HEAD_DIM = 128
N_KV_HEADS = 2
WINDOW_BLOCK = 128
GRID_W = 64
ROPE_BASE = 10000.0
ROPE_FREQS = HEAD_DIM // 4
CONV_LEFT = 2
RGLRU_C = 8.0
N_MOD = 9
EPS = 1e-6
MOD_ROWS = 16
NEG_BIG = -1e30
LOG2_E = 1.4426950408889634

TM_FFN = 1024
TF = 512
TN_PROJ = 1536
TM_MERGE = 512
MERGE_GATE_COLS = 512
TN_ADA = 1024
NORM_ROWS = 128
NORM_UNROLL = 4
SCAN_CHUNK = 256
SCAN_SUB = 64
SCAN_WIDTH = 512
ATTN_Q_BLOCKS = 8
CAST_ROWS = 256


def _cparams(sem, block_bytes):
    limit = min(int(block_bytes) + VMEM_HEADROOM_BYTES, VMEM_PHYSICAL_BYTES - VMEM_HEADROOM_BYTES // 2)
    return pltpu.CompilerParams(dimension_semantics=sem, vmem_limit_bytes=limit)


def _adaln_kernel(c_ref, w_ref, b_ref, o_ref):
    c = c_ref[...]
    sc = c * jax.nn.sigmoid(c)
    o_ref[...] = jnp.dot(sc, w_ref[...], preferred_element_type=F32,
                         precision=lax.Precision.HIGHEST) + b_ref[...]


def _adaln(c_all, w_ada, b_ada):
    depth, d, n = w_ada.shape
    return pl.pallas_call(
        _adaln_kernel,
        out_shape=jax.ShapeDtypeStruct((depth, MOD_ROWS, n), F32),
        grid=(depth, n // TN_ADA),
        in_specs=[
            pl.BlockSpec((MOD_ROWS, d), lambda l, j: (0, 0)),
            pl.BlockSpec((None, d, TN_ADA), lambda l, j: (l, 0, j)),
            pl.BlockSpec((None, 1, TN_ADA), lambda l, j: (l, 0, j)),
        ],
        out_specs=pl.BlockSpec((None, MOD_ROWS, TN_ADA), lambda l, j: (l, 0, j)),
        compiler_params=_cparams(("parallel", "parallel"), 2 * d * TN_ADA * 4 + 4 * MOD_ROWS * (d + TN_ADA) * 4),
        name="adaln",
    )(c_all, w_ada, b_ada.reshape(depth, 1, n))


def _rms_scale(x):
    return x * lax.rsqrt(jnp.mean(x * x, axis=-1, keepdims=True) + EPS)


def _norm_mod_store(xn_ref, rs_ref, x_ref, g_ref, sh_ref, sc_ref, zero_ref=None):
    n_steps = x_ref.shape[0] // NORM_ROWS
    inv_d = 1.0 / x_ref.shape[1]
    gs = g_ref[...] * (1.0 + sc_ref[...])
    sh = sh_ref[...]

    def stats(i, carry):
        r0 = pl.multiple_of(i * NORM_ROWS, NORM_ROWS)
        x = x_ref[pl.ds(r0, NORM_ROWS), :]
        rs_ref[pl.ds(r0, NORM_ROWS), :] = lax.rsqrt(jnp.sum(x * x, axis=-1, keepdims=True) * inv_d + EPS)
        return carry

    lax.fori_loop(0, n_steps, stats, 0, unroll=NORM_UNROLL)

    def scale(i, carry):
        r0 = pl.multiple_of(i * NORM_ROWS, NORM_ROWS)
        y = (x_ref[pl.ds(r0, NORM_ROWS), :] * rs_ref[pl.ds(r0, NORM_ROWS), :]) * gs + sh
        xn_ref[pl.ds(r0, NORM_ROWS), :] = y.astype(xn_ref.dtype)
        if zero_ref is not None:
            zero_ref[pl.ds(r0, NORM_ROWS), :] = jnp.zeros((NORM_ROWS, zero_ref.shape[1]), zero_ref.dtype)
        return carry

    lax.fori_loop(0, n_steps, scale, 0)


def _mod_spec(layer, k, row_of_tile, d):
    return pl.BlockSpec((None, None, 1, d), lambda i, j: (layer, row_of_tile(i), 0, k))


def _ffn_kernel(x_ref, g_ref, sh_ref, sc_ref, gt_ref, wg_ref, wu_ref, wd_ref, *refs, final_norm):
    if final_norm:
        fg_ref, o_ref, xn_ref, rs_ref = refs
    else:
        o_ref, xn_ref, rs_ref = refs
    c = pl.program_id(1)

    @pl.when(c == 0)
    def _():
        _norm_mod_store(xn_ref, rs_ref, x_ref, g_ref, sh_ref, sc_ref, zero_ref=o_ref)

    xn = xn_ref[...]
    h = jnp.dot(xn, wg_ref[...], preferred_element_type=F32)
    u = jnp.dot(xn, wu_ref[...], preferred_element_type=F32)
    a = ((h * jax.nn.sigmoid(h)) * u).astype(BF16)
    o_ref[...] += jnp.dot(a, wd_ref[...], preferred_element_type=F32)

    @pl.when(c == pl.num_programs(1) - 1)
    def _():
        gate = 0.5 * gt_ref[...]

        def body(i, carry):
            r0 = pl.multiple_of(i * NORM_ROWS, NORM_ROWS)
            y = x_ref[pl.ds(r0, NORM_ROWS), :] + gate * o_ref[pl.ds(r0, NORM_ROWS), :]
            if final_norm:
                y = _rms_scale(y) * fg_ref[...]
            o_ref[pl.ds(r0, NORM_ROWS), :] = y
            return carry

        lax.fori_loop(0, x_ref.shape[0] // NORM_ROWS, body, 0)


def _ffn(xs, mods, layer, half, g, wg, wu, wd, row_of_tile, final_g=None, in_place=True):
    rows, d = xs.shape
    ff = wg.shape[-1]
    tm = TM_FFN
    mod_k0 = 6 * half
    final_norm = final_g is not None
    in_specs = [
        pl.BlockSpec((tm, d), lambda i, c: (i, 0)),
        pl.BlockSpec((1, d), lambda i, c: (0, 0)),
        _mod_spec(layer, mod_k0, row_of_tile, d),
        _mod_spec(layer, mod_k0 + 1, row_of_tile, d),
        _mod_spec(layer, mod_k0 + 2, row_of_tile, d),
        pl.BlockSpec((None, None, d, TF), lambda i, c: (layer, half, 0, c)),
        pl.BlockSpec((None, None, d, TF), lambda i, c: (layer, half, 0, c)),
        pl.BlockSpec((None, None, TF, d), lambda i, c: (layer, half, c, 0)),
    ]
    args = [xs, g.reshape(1, d), mods, mods, mods, wg, wu, wd]
    if final_norm:
        in_specs.append(pl.BlockSpec((1, d), lambda i, c: (0, 0)))
        args.append(final_g.reshape(1, d))
    block_bytes = (2 * tm * d * 4 + 2 * tm * d * 4 + 2 * 3 * d * TF * 2 + tm * d * 2
                   + 2 * tm * TF * 4 + tm * TF * 2)
    return pl.pallas_call(
        functools.partial(_ffn_kernel, final_norm=final_norm),
        out_shape=jax.ShapeDtypeStruct((rows, d), F32),
        grid=(rows // tm, ff // TF),
        in_specs=in_specs,
        out_specs=pl.BlockSpec((tm, d), lambda i, c: (i, 0)),
        scratch_shapes=[pltpu.VMEM((tm, d), BF16), pltpu.VMEM((tm, 1), F32)],
        input_output_aliases={0: 0} if (in_place and not final_norm) else {},
        compiler_params=_cparams(("parallel", "arbitrary"), block_bytes),
        name="ffn_final" if final_norm else "ffn",
    )(*args)


def _rope(t, cs, sn):
    lane = lax.broadcasted_iota(jnp.int32, t.shape, 1)
    first_half = (lane & (2 * ROPE_FREQS - 1)) < ROPE_FREQS
    partner = jnp.where(first_half,
                        pltpu.roll(t, HEAD_DIM - ROPE_FREQS, axis=1),
                        pltpu.roll(t, ROPE_FREQS, axis=1))
    return t * cs + partner * sn


def _proj_kernel(x_ref, g_ref, sh_ref, sc_ref, w_ref, *refs, rope_cols, q_cols, q_scale):
    if rope_cols:
        cos_ref, sin_ref, o_ref, xn_ref, rs_ref = refs
    else:
        o_ref, xn_ref, rs_ref = refs
    j = pl.program_id(1)

    @pl.when(j == 0)
    def _():
        _norm_mod_store(xn_ref, rs_ref, x_ref, g_ref, sh_ref, sc_ref)

    def plain():
        o_ref[...] = jnp.dot(xn_ref[...], w_ref[...], preferred_element_type=F32).astype(o_ref.dtype)

    if not rope_cols:
        plain()
        return

    @pl.when(j == 0)
    def _():
        xn = xn_ref[...]
        cs = cos_ref[...]
        sn = sin_ref[...]
        pair = 2 * HEAD_DIM
        for c0 in range(0, o_ref.shape[1], pair):
            res = jnp.dot(xn, w_ref[:, c0:c0 + pair], preferred_element_type=F32)
            for h0 in range(0, pair, HEAD_DIM):
                t = res[:, h0:h0 + HEAD_DIM]
                if c0 + h0 < rope_cols:
                    t = _rope(t, cs, sn)
                if c0 + h0 < q_cols:
                    t = t * q_scale
                o_ref[:, c0 + h0:c0 + h0 + HEAD_DIM] = t.astype(o_ref.dtype)

    pl.when(j != 0)(plain)


def _proj(xs, mods, layer, g, w, row_of_tile, rope=None):
    rows, d = xs.shape
    n = w.shape[-1]
    tm = TM_FFN
    in_specs = [
        pl.BlockSpec((tm, d), lambda i, j: (i, 0)),
        pl.BlockSpec((1, d), lambda i, j: (0, 0)),
        _mod_spec(layer, 3, row_of_tile, d),
        _mod_spec(layer, 4, row_of_tile, d),
        pl.BlockSpec((None, d, TN_PROJ), lambda i, j: (layer, 0, j)),
    ]
    args = [xs, g.reshape(1, d), mods, mods, w]
    rope_cols = q_cols = 0
    q_scale = 1.0
    if rope is not None:
        cos_t, sin_t, rope_cols, q_cols, q_scale, tiles_per_sample = rope
        assert q_cols <= rope_cols <= TN_PROJ
        in_specs += [pl.BlockSpec((tm, HEAD_DIM), lambda i, j: (i % tiles_per_sample, 0))] * 2
        args += [cos_t, sin_t]
    block_bytes = (2 * tm * d * 4 + tm * d * 2 + 2 * d * TN_PROJ * 2 + 2 * tm * TN_PROJ * 2
                   + tm * TN_PROJ * 4 + 4 * tm * HEAD_DIM * 4)
    return pl.pallas_call(
        functools.partial(_proj_kernel, rope_cols=rope_cols, q_cols=q_cols, q_scale=q_scale),
        out_shape=jax.ShapeDtypeStruct((rows, n), BF16),
        grid=(rows // tm, n // TN_PROJ),
        in_specs=in_specs,
        out_specs=pl.BlockSpec((tm, TN_PROJ), lambda i, j: (i, j)),
        scratch_shapes=[pltpu.VMEM((tm, d), BF16), pltpu.VMEM((tm, 1), F32)],
        compiler_params=_cparams(("parallel", "arbitrary"), block_bytes),
        name="in_proj_rope" if rope_cols else "in_proj",
    )(*args)


def _scores(q4, kcat):
    return lax.dot_general(q4, kcat, (((1,), (1,)), ((), ())), preferred_element_type=F32)


def _softmax_pv(s, vcat, sk, bias, scale):
    if scale is None:
        sk = sk * LOG2_E
        exp = jnp.exp2
    else:
        s = s * scale
        exp = jnp.exp
    if bias is not None:
        rb = bias.shape[0]
        s = jnp.concatenate([s[r0:r0 + rb] + bias for r0 in range(0, s.shape[0], rb)], axis=0)
    m = jnp.maximum(jnp.max(s, axis=-1, keepdims=True), sk)
    p = exp(s - m)
    denom = jnp.sum(p, axis=-1, keepdims=True) + exp(sk - m)
    o = jnp.dot(p.astype(BF16), vcat, preferred_element_type=F32)
    return o / denom


def _sink_rows(sink_ref, g, q_per_kv, rows):
    parts = [jnp.broadcast_to(sink_ref[g * q_per_kv + j:g * q_per_kv + j + 1, 0:1], (rows, 1))
             for j in range(q_per_kv)]
    return jnp.concatenate(parts, axis=0)


def _attn_lat_kernel(q_ref, kp_ref, kc_ref, kn_ref, vp_ref, vc_ref, vn_ref, kx_ref, vx_ref,
                     sink_ref, o_ref, *, nb, q_per_kv, scale):
    n = pl.program_id(1)
    blk = WINDOW_BLOCK
    dh = HEAD_DIM
    c_len = kx_ref.shape[0]
    nq = q_ref.shape[0] // blk
    keys = 3 * blk + c_len
    ri = lax.broadcasted_iota(jnp.int32, (blk, keys), 0)
    kj = lax.broadcasted_iota(jnp.int32, (blk, keys), 1)
    band = (kj >= 3 * blk) | ((kj >= ri) & (kj <= ri + 2 * blk))

    def window(qb, p_ref, c_ref, n_ref, x_ref, cols):
        parts = []
        for w in (qb - 1, qb, qb + 1):
            if w < 0:
                parts.append(p_ref[:, cols])
            elif w >= nq:
                parts.append(n_ref[:, cols])
            else:
                parts.append(c_ref[w * blk:(w + 1) * blk, cols])
        return jnp.concatenate(parts + [x_ref[:, cols]], axis=0)

    def scores(qb, g):
        q4 = jnp.concatenate([q_ref[qb * blk:(qb + 1) * blk, (g * q_per_kv + j) * dh:(g * q_per_kv + j + 1) * dh]
                              for j in range(q_per_kv)], axis=0)
        return _scores(q4, window(qb, kp_ref, kc_ref, kn_ref, kx_ref, slice(g * dh, (g + 1) * dh)))

    streams = [(qb, g) for qb in range(nq) for g in range(N_KV_HEADS)]
    s_next = scores(*streams[0])
    for i, (qb, g) in enumerate(streams):
        s_cur = s_next
        if i + 1 < len(streams):
            s_next = scores(*streams[i + 1])
        first = (n == 0) if qb == 0 else False
        last = (n == nb // nq - 1) if qb == nq - 1 else False
        lo = jnp.where(first, blk, 0)
        hi = jnp.where(last, 2 * blk, 3 * blk)
        bias = jnp.where(band & ((kj >= 3 * blk) | ((kj >= lo) & (kj < hi))), 0.0, NEG_BIG).astype(F32)
        vcat = window(qb, vp_ref, vc_ref, vn_ref, vx_ref, slice(g * dh, (g + 1) * dh))
        o = _softmax_pv(s_cur, vcat, _sink_rows(sink_ref, g, q_per_kv, blk), bias, scale)
        for j in range(q_per_kv):
            h = g * q_per_kv + j
            o_ref[qb * blk:(qb + 1) * blk, h * dh:(h + 1) * dh] = o[j * blk:(j + 1) * blk].astype(o_ref.dtype)


def _attn_ctx_kernel(q_ref, kx_ref, vx_ref, sink_ref, o_ref, *, q_per_kv, scale):
    dh = HEAD_DIM
    c_len = q_ref.shape[0]
    for g in range(N_KV_HEADS):
        cols = slice(g * dh, (g + 1) * dh)
        q4 = jnp.concatenate([q_ref[:, (g * q_per_kv + j) * dh:(g * q_per_kv + j + 1) * dh]
                              for j in range(q_per_kv)], axis=0)
        sk = _sink_rows(sink_ref, g, q_per_kv, c_len)
        o = _softmax_pv(_scores(q4, kx_ref[:, cols]), vx_ref[:, cols], sk, None, scale)
        for j in range(q_per_kv):
            h = g * q_per_kv + j
            o_ref[:, h * dh:(h + 1) * dh] = o[j * c_len:(j + 1) * c_len].astype(o_ref.dtype)


def _attention(p_lat, p_ctx, sink_b, bsz, s_len, c_len, attn_w, k_off, need_ctx):
    kv_w = N_KV_HEADS * HEAD_DIM
    n_q_heads = attn_w // HEAD_DIM
    q_per_kv = n_q_heads // N_KV_HEADS
    blk = WINDOW_BLOCK
    nb = s_len // blk
    k_col = k_off // kv_w
    v_col = k_col + 1
    scale = HEAD_DIM ** -0.5

    nq = ATTN_Q_BLOCKS
    steps = nb // nq
    assert nb % nq == 0

    def attn_bytes(q_rows):
        blocks = 2 * 2 * (2 * q_rows * attn_w + 2 * (q_rows + 2 * blk + c_len) * kv_w)
        return blocks + 8 * q_per_kv * max(blk, c_len) * (3 * blk + c_len) * 4

    def kv_specs(col):
        prev = pl.BlockSpec((blk, kv_w), lambda b, n: (b * nb + jnp.maximum(n * nq - 1, 0), col))
        own = pl.BlockSpec((nq * blk, kv_w), lambda b, n: (b * steps + n, col))
        nxt = pl.BlockSpec((blk, kv_w), lambda b, n: (b * nb + jnp.minimum((n + 1) * nq, nb - 1), col))
        return [prev, own, nxt]

    ya_lat = pl.pallas_call(
        functools.partial(_attn_lat_kernel, nb=nb, q_per_kv=q_per_kv, scale=None),
        out_shape=jax.ShapeDtypeStruct((bsz * s_len, attn_w), BF16),
        grid=(bsz, steps),
        in_specs=[pl.BlockSpec((nq * blk, attn_w), lambda b, n: (b * steps + n, 0))]
        + kv_specs(k_col) + kv_specs(v_col) + [
            pl.BlockSpec((c_len, kv_w), lambda b, n: (b, k_col)),
            pl.BlockSpec((c_len, kv_w), lambda b, n: (b, v_col)),
            pl.BlockSpec((n_q_heads, LANES), lambda b, n: (0, 0)),
        ],
        out_specs=pl.BlockSpec((nq * blk, attn_w), lambda b, n: (b * steps + n, 0)),
        compiler_params=_cparams(("parallel", "parallel"), attn_bytes(nq * blk)),
        name="attn_latent",
    )(p_lat, p_lat, p_lat, p_lat, p_lat, p_lat, p_lat, p_ctx, p_ctx, sink_b)
    if not need_ctx:
        return ya_lat, None
    ya_ctx = pl.pallas_call(
        functools.partial(_attn_ctx_kernel, q_per_kv=q_per_kv, scale=scale),
        out_shape=jax.ShapeDtypeStruct((bsz * c_len, attn_w), BF16),
        grid=(bsz,),
        in_specs=[
            pl.BlockSpec((c_len, attn_w), lambda b: (b, 0)),
            pl.BlockSpec((c_len, kv_w), lambda b: (b, k_col)),
            pl.BlockSpec((c_len, kv_w), lambda b: (b, v_col)),
            pl.BlockSpec((n_q_heads, LANES), lambda b: (0, 0)),
        ],
        out_specs=pl.BlockSpec((c_len, attn_w), lambda b: (b, 0)),
        compiler_params=_cparams(("parallel",), attn_bytes(c_len)),
        name="attn_context",
    )(p_ctx, p_ctx, p_ctx, sink_b)
    return ya_lat, ya_ctx


def _rglru_kernel(xp_ref, x_ref, xn_ref, cw_ref, cb_ref, w_ref, b_ref, lam_ref, h0_ref, *refs,
                  reverse, nk):
    if reverse:
        hf_ref, y_ref, hend_ref, xt_ref, h_ref, hs_ref = refs
    else:
        y_ref, hend_ref, xt_ref, h_ref = refs
    k = pl.program_id(1)
    ci = nk - 1 - k if reverse else k
    bsz, tc, width = x_ref.shape
    n_lb = width // LANES
    halo = xp_ref.shape[1]
    n_sub = tc // SCAN_SUB
    rows_sub = SCAN_SUB * bsz

    @pl.when(k == 0)
    def _():
        h_ref[...] = h0_ref[...]

    for c in range(n_lb):
        lanes = slice(c * LANES, (c + 1) * LANES)
        for b in range(bsz):
            xt_ref[c, pl.ds(b, halo, stride=bsz), :] = xp_ref[b, :, lanes].astype(F32)
            xt_ref[c, pl.ds(halo * bsz + b, tc, stride=bsz), :] = x_ref[b, :, lanes].astype(F32)
            xt_ref[c, pl.ds((halo + tc) * bsz + b, halo, stride=bsz), :] = xn_ref[b, :, lanes].astype(F32)

    @pl.when(ci == 0)
    def _():
        xt_ref[:, 0:halo * bsz, :] = jnp.zeros((n_lb, halo * bsz, LANES), F32)

    @pl.when(ci == nk - 1)
    def _():
        xt_ref[:, (halo + tc) * bsz:(2 * halo + tc) * bsz, :] = jnp.zeros((n_lb, halo * bsz, LANES), F32)

    def sub_body(i, hs):
        si = (n_sub - 1 - i) if reverse else i
        t0 = si * SCAN_SUB
        new_h = []
        for c in range(n_lb):
            lanes = slice(c * LANES, (c + 1) * LANES)
            cw = cw_ref[:, lanes]
            u = cb_ref[:, lanes]
            for kk in range(cw.shape[0]):
                r0 = pl.multiple_of((t0 + halo - CONV_LEFT + kk) * bsz, bsz)
                u = u + cw[kk:kk + 1, :] * xt_ref[c, pl.ds(r0, rows_sub), :]
            g = jnp.dot(u.astype(BF16), w_ref[c], preferred_element_type=F32)
            gr = 1.0 / (1.0 + jnp.exp2(g[:, :LANES] - LOG2_E * b_ref[0:1, lanes]))
            gi = 1.0 / (1.0 + jnp.exp2(g[:, LANES:] - LOG2_E * b_ref[1:2, lanes]))
            neg = -lam_ref[:, lanes]
            softplus = jnp.maximum(neg, 0.0) + jnp.log1p(jnp.exp(-jnp.abs(neg)))
            log_a = gr * ((-RGLRU_C) * softplus)
            a = jnp.exp(log_a)
            x1 = -jnp.tanh(log_a) * (a * a + 1.0)
            m = jnp.where(x1 > 0.0, x1 * lax.rsqrt(x1), 0.0)
            bx = m * (gi * u)
            h = hs[c]
            outs = [None] * SCAN_SUB
            for t in (range(SCAN_SUB - 1, -1, -1) if reverse else range(SCAN_SUB)):
                h = a[t * bsz:(t + 1) * bsz] * h + bx[t * bsz:(t + 1) * bsz]
                outs[t] = h
            rows = pl.ds(pl.multiple_of(t0 * bsz, rows_sub), rows_sub)
            if reverse:
                hs_ref[c, rows, :] = jnp.concatenate(outs, axis=0) + hf_ref[rows, lanes]
            else:
                y_ref[rows, lanes] = jnp.concatenate(outs, axis=0)
            new_h.append(h)
        return tuple(new_h)

    h_init = tuple(h_ref[:, c * LANES:(c + 1) * LANES] for c in range(n_lb))
    h_fin = lax.fori_loop(0, n_sub, sub_body, h_init)
    for c in range(n_lb):
        lanes = slice(c * LANES, (c + 1) * LANES)
        h_ref[:, lanes] = h_fin[c]
        if reverse:
            for b in range(bsz):
                y_ref[b, :, lanes] = hs_ref[c, pl.ds(b, tc, stride=bsz), :].astype(y_ref.dtype)

    @pl.when(k == nk - 1)
    def _():
        hend_ref[...] = h_ref[...]


def _rglru_dir(p3, col0, layer, conv_w, conv_b, w_gates, b_gates, lam, h0, hf, reverse):
    direction = 1 if reverse else 0
    bsz, seq, _ = p3.shape
    rnn_w = conv_w.shape[-1]
    width = SCAN_WIDTH
    tc = min(SCAN_CHUNK, seq)
    nk = seq // tc
    n_lb = width // LANES
    cblk0 = col0 // width
    halo = 2 * SUBLANES
    hpt = tc // halo
    n_halo_blocks = seq // halo

    def tmap(k):
        return nk - 1 - k if reverse else k

    in_specs = [
        pl.BlockSpec((bsz, halo, width), lambda j, k: (0, jnp.maximum(tmap(k) * hpt - 1, 0), cblk0 + j)),
        pl.BlockSpec((bsz, tc, width), lambda j, k: (0, tmap(k), cblk0 + j)),
        pl.BlockSpec((bsz, halo, width),
                     lambda j, k: (0, jnp.minimum((tmap(k) + 1) * hpt, n_halo_blocks - 1), cblk0 + j)),
        pl.BlockSpec((conv_w.shape[0], width), lambda j, k: (0, j)),
        pl.BlockSpec((1, width), lambda j, k: (0, j)),
        pl.BlockSpec((None, None, n_lb, LANES, 2 * LANES), lambda j, k: (layer, direction, j, 0, 0)),
        pl.BlockSpec((2, width), lambda j, k: (0, j)),
        pl.BlockSpec((1, width), lambda j, k: (0, j)),
        pl.BlockSpec((bsz, width), lambda j, k: (0, j)),
    ]
    args = [p3, p3, p3, conv_w, conv_b.reshape(1, rnn_w), w_gates, b_gates, lam.reshape(1, rnn_w), h0]
    time_major = pl.BlockSpec((tc * bsz, width), lambda j, k: (tmap(k), j))
    scratch = [pltpu.VMEM((n_lb, (tc + 2 * halo) * bsz, LANES), F32), pltpu.VMEM((bsz, width), F32)]
    if reverse:
        in_specs.append(time_major)
        args.append(hf)
        y_shape = jax.ShapeDtypeStruct((bsz, seq, rnn_w), BF16)
        y_spec = pl.BlockSpec((bsz, tc, width), lambda j, k: (0, tmap(k), j))
        scratch.append(pltpu.VMEM((n_lb, tc * bsz, LANES), F32))
    else:
        y_shape = jax.ShapeDtypeStruct((seq * bsz, rnn_w), F32)
        y_spec = time_major
    block_bytes = (2 * bsz * tc * width * 2 + 2 * bsz * tc * width * 4 * 2
                   + (tc + 2 * halo) * bsz * width * 4 + tc * bsz * width * 4)
    return pl.pallas_call(
        functools.partial(_rglru_kernel, reverse=reverse, nk=nk),
        out_shape=(y_shape, jax.ShapeDtypeStruct((bsz, rnn_w), F32)),
        grid=(rnn_w // width, nk),
        in_specs=in_specs,
        out_specs=(y_spec, pl.BlockSpec((bsz, width), lambda j, k: (0, j))),
        scratch_shapes=scratch,
        compiler_params=_cparams(("parallel", "arbitrary"), block_bytes),
        name="rglru_bwd" if reverse else "rglru_fwd",
    )(*args)


def _rglru(p3, col0, layer, conv_w, conv_b, w_gates, b_gates, lam, h0):
    hf, end_f = _rglru_dir(p3, col0, layer, conv_w, conv_b, w_gates, b_gates[0], lam[0], h0[0], None, False)
    y, end_b = _rglru_dir(p3, col0, layer, conv_w, conv_b, w_gates, b_gates[1], lam[1], h0[1], hf, True)
    return y, (end_f, end_b)


def _gelu_tanh(x):
    return 0.5 * x * (1.0 + jnp.tanh(0.7978845608028654 * (x + 0.044715 * (x * x * x))))


def _merge_kernel(x_ref, ya_ref, r_ref, *refs, n_gr, n_g):
    gr, ga, gb = refs[:n_gr], refs[n_gr:n_gr + n_g], refs[n_gr + n_g:n_gr + 2 * n_g]
    gt_ref, woa_ref, wor_ref, wout_ref, o_ref = refs[n_gr + 2 * n_g:]

    def wide(blocks):
        return jnp.concatenate([blk[...] for blk in blocks], axis=1).astype(F32)

    yr = (r_ref[...].astype(F32) * _gelu_tanh(wide(gr))).astype(BF16)
    pa = jnp.dot(ya_ref[...], woa_ref[...], preferred_element_type=F32)
    pr = jnp.dot(yr, wor_ref[...], preferred_element_type=F32)
    t = jax.nn.sigmoid(wide(ga)) * pa + jax.nn.sigmoid(wide(gb)) * pr
    o_ref[...] = x_ref[...] + gt_ref[...] * jnp.dot(t.astype(BF16), wout_ref[...], preferred_element_type=F32)


def _merge(xs, ya, r, proj, gr_off, ga_off, gb_off, mods, layer, woa, wor, wout, row_of_tile):
    rows, d = xs.shape
    attn_w = ya.shape[1]
    rnn_w = r.shape[1]
    tm = TM_MERGE
    gw = MERGE_GATE_COLS
    assert gr_off % gw == 0 and ga_off % gw == 0 and gb_off % gw == 0 and rnn_w % gw == 0 and d % gw == 0
    n_gr, n_g = rnn_w // gw, d // gw

    def cols(off, n_blocks):
        return [pl.BlockSpec((tm, gw), lambda i, k=k: (i, off // gw + k)) for k in range(n_blocks)]

    def weight_spec(k_rows):
        return pl.BlockSpec((None, k_rows, d), lambda i: (layer, 0, 0), pipeline_mode=pl.Buffered(1))

    block_bytes = (4 * tm * d * 4 + 2 * tm * (attn_w + 2 * rnn_w + 2 * d) * 2 + (attn_w + rnn_w + d) * d * 2
                   + 4 * tm * d * 4)
    return pl.pallas_call(
        functools.partial(_merge_kernel, n_gr=n_gr, n_g=n_g),
        out_shape=jax.ShapeDtypeStruct((rows, d), F32),
        grid=(rows // tm,),
        in_specs=[
            pl.BlockSpec((tm, d), lambda i: (i, 0)),
            pl.BlockSpec((tm, attn_w), lambda i: (i, 0)),
            pl.BlockSpec((tm, rnn_w), lambda i: (i, 0)),
        ] + cols(gr_off, n_gr) + cols(ga_off, n_g) + cols(gb_off, n_g) + [
            pl.BlockSpec((None, None, 1, d), lambda i: (layer, row_of_tile(i), 0, 5)),
            weight_spec(attn_w), weight_spec(rnn_w), weight_spec(d),
        ],
        out_specs=pl.BlockSpec((tm, d), lambda i: (i, 0)),
        input_output_aliases={0: 0},
        compiler_params=_cparams(("parallel",), block_bytes),
        name="merge",
    )(xs, ya, r, *([proj] * (n_gr + 2 * n_g)), mods, woa, wor, wout)


def _up_cast_kernel(w_ref, g_ref, u_ref, *, d_ff):
    pad = g_ref.shape[1] - d_ff
    g_ref[:, :d_ff] = w_ref[:, :d_ff].astype(g_ref.dtype)
    u_ref[:, :d_ff] = w_ref[:, d_ff:].astype(u_ref.dtype)
    if pad:
        g_ref[:, d_ff:] = jnp.zeros((g_ref.shape[0], pad), g_ref.dtype)
        u_ref[:, d_ff:] = jnp.zeros((u_ref.shape[0], pad), u_ref.dtype)


def _ffn_up_weights(w_up, d_ff, ff_pad):
    depth, halves, d, n = w_up.shape
    assert n == 2 * d_ff and d_ff % LANES == 0 and d % CAST_ROWS == 0
    out = jax.ShapeDtypeStruct((depth * halves, d, ff_pad), BF16)
    wg, wu = pl.pallas_call(
        functools.partial(_up_cast_kernel, d_ff=d_ff),
        out_shape=(out, out),
        grid=(depth * halves, d // CAST_ROWS),
        in_specs=[pl.BlockSpec((None, CAST_ROWS, n), lambda l, r: (l, r, 0))],
        out_specs=(pl.BlockSpec((None, CAST_ROWS, ff_pad), lambda l, r: (l, r, 0)),
                   pl.BlockSpec((None, CAST_ROWS, ff_pad), lambda l, r: (l, r, 0))),
        compiler_params=_cparams(("parallel", "parallel"), 2 * CAST_ROWS * (n * 4 + 2 * ff_pad * 2)),
        name="ffn_up_cast",
    )(w_up.reshape(depth * halves, d, n))
    return wg.reshape(depth, halves, d, ff_pad), wu.reshape(depth, halves, d, ff_pad)


def _down_cast_kernel(w_ref, o_ref, *, tail_rows):
    r = pl.program_id(1)
    last = pl.num_programs(1) - 1

    @pl.when(r != last)
    def _():
        o_ref[...] = w_ref[...].astype(o_ref.dtype)

    @pl.when(r == last)
    def _():
        o_ref[:tail_rows, :] = w_ref[:tail_rows, :].astype(o_ref.dtype)
        if tail_rows < o_ref.shape[0]:
            o_ref[tail_rows:, :] = jnp.zeros((o_ref.shape[0] - tail_rows, o_ref.shape[1]), o_ref.dtype)


def _ffn_down_weights(w_down, ff_pad):
    depth, halves, d_ff, d = w_down.shape
    n_blocks = ff_pad // TF
    tail_rows = d_ff - (n_blocks - 1) * TF
    assert 0 < tail_rows <= TF and tail_rows % (2 * SUBLANES) == 0
    wd = pl.pallas_call(
        functools.partial(_down_cast_kernel, tail_rows=tail_rows),
        out_shape=jax.ShapeDtypeStruct((depth * halves, ff_pad, d), BF16),
        grid=(depth * halves, n_blocks),
        in_specs=[pl.BlockSpec((None, TF, d), lambda l, r: (l, r, 0))],
        out_specs=pl.BlockSpec((None, TF, d), lambda l, r: (l, r, 0)),
        compiler_params=_cparams(("parallel", "parallel"), 2 * TF * d * (4 + 2)),
        name="ffn_down_cast",
    )(w_down.reshape(depth * halves, d_ff, d))
    return wd.reshape(depth, halves, ff_pad, d)


def _rope_tables(s_len):
    pos = jnp.arange(s_len, dtype=jnp.int32)
    row = (pos // GRID_W).astype(F32)
    col = (pos % GRID_W).astype(F32)
    inv_freq = ROPE_BASE ** (-jnp.arange(ROPE_FREQS, dtype=F32) / ROPE_FREQS)
    ang_r = row[:, None] * inv_freq
    ang_c = col[:, None] * inv_freq
    cos_t = jnp.concatenate([jnp.cos(ang_r), jnp.cos(ang_r), jnp.cos(ang_c), jnp.cos(ang_c)], axis=1)
    sin_t = jnp.concatenate([-jnp.sin(ang_r), jnp.sin(ang_r), -jnp.sin(ang_c), jnp.sin(ang_c)], axis=1)
    return cos_t, sin_t


def kernel(x, c, ctx, c_ctx, w_ada, b_ada, norm_g, final_g, w_ffn_up, w_ffn_down, w_in, attn_sink,
           conv_w, conv_b, rg_w, rg_b, rg_lambda, w_o_attn, w_o_rnn, w_out):
    bsz, s_len, d = x.shape
    c_len = ctx.shape[1]
    depth = w_ada.shape[0]
    d_ff = w_ffn_down.shape[2]
    attn_w = w_o_attn.shape[1]
    rnn_w = w_o_rnn.shape[1]
    kv_w = N_KV_HEADS * HEAD_DIM
    ff_pad = -(-d_ff // TF) * TF
    n_lat, n_ctx = bsz * s_len, bsz * c_len
    assert s_len % TM_FFN == 0 and n_ctx % TM_FFN == 0 and s_len % TM_MERGE == 0 and n_ctx % TM_MERGE == 0
    assert bsz == SUBLANES and bsz + 1 <= MOD_ROWS and rnn_w % SCAN_WIDTH == 0
    assert s_len % SCAN_CHUNK == 0 and c_len % SCAN_SUB == 0 and c_len <= SCAN_CHUNK

    off_k = attn_w
    off_xr = off_k + 2 * kv_w
    off_gr = off_xr + rnn_w
    off_ga = off_gr + rnn_w
    off_gb = off_ga + d
    assert off_xr == TN_PROJ and off_k % kv_w == 0 and off_xr % SCAN_WIDTH == 0

    def lat_row(tm):
        return lambda i: i // (s_len // tm)

    def ctx_row(i):
        return bsz

    wg_all, wu_all = _ffn_up_weights(w_ffn_up, d_ff, ff_pad)
    wd_all = _ffn_down_weights(w_ffn_down, ff_pad)
    w_in_all = w_in.astype(BF16)
    woa_all, wor_all, wout_all = w_o_attn.astype(BF16), w_o_rnn.astype(BF16), w_out.astype(BF16)
    w_gates_all = (-LOG2_E * jnp.concatenate([rg_w[:, :, 0], rg_w[:, :, 1]], axis=-1)).astype(BF16)

    c_all = jnp.zeros((MOD_ROWS, d), F32).at[:bsz].set(c).at[bsz].set(c_ctx)
    mods = _adaln(c_all, w_ada, b_ada).reshape(depth, MOD_ROWS, 1, N_MOD * d)
    cos_t, sin_t = _rope_tables(s_len)
    rope = (cos_t, sin_t, attn_w + kv_w, attn_w, HEAD_DIM ** -0.5 * LOG2_E, s_len // TM_FFN)
    xl = x.reshape(n_lat, d)
    xc = ctx.reshape(n_ctx, d)
    h_zero = (jnp.zeros((bsz, rnn_w), F32), jnp.zeros((bsz, rnn_w), F32))

    for l in range(depth):
        need_ctx = l < depth - 1
        first = l == 0
        xl = _ffn(xl, mods, l, 0, norm_g[l, 0], wg_all, wu_all, wd_all, lat_row(TM_FFN), in_place=not first)
        xc = _ffn(xc, mods, l, 0, norm_g[l, 0], wg_all, wu_all, wd_all, ctx_row, in_place=not first)

        p_lat = _proj(xl, mods, l, norm_g[l, 1], w_in_all, lat_row(TM_FFN), rope=rope)
        p_ctx = _proj(xc, mods, l, norm_g[l, 1], w_in_all, ctx_row)

        sink_b = jnp.broadcast_to(attn_sink[l][:, None], (attn_sink.shape[1], LANES))
        ya_lat, ya_ctx = _attention(p_lat, p_ctx, sink_b, bsz, s_len, c_len, attn_w, off_k, need_ctx)

        r_ctx, h_end = _rglru(p_ctx.reshape(bsz, c_len, -1), off_xr, l, conv_w[l], conv_b[l], w_gates_all,
                              rg_b[l], rg_lambda[l], h_zero)
        r_lat, _ = _rglru(p_lat.reshape(bsz, s_len, -1), off_xr, l, conv_w[l], conv_b[l], w_gates_all,
                          rg_b[l], rg_lambda[l], h_end)

        xl = _merge(xl, ya_lat, r_lat.reshape(n_lat, rnn_w), p_lat, off_gr, off_ga, off_gb, mods, l,
                    woa_all, wor_all, wout_all, lat_row(TM_MERGE))
        if need_ctx:
            xc = _merge(xc, ya_ctx, r_ctx.reshape(n_ctx, rnn_w), p_ctx, off_gr, off_ga, off_gb, mods, l,
                        woa_all, wor_all, wout_all, ctx_row)
            xc = _ffn(xc, mods, l, 1, norm_g[l, 2], wg_all, wu_all, wd_all, ctx_row)
        xl = _ffn(xl, mods, l, 1, norm_g[l, 2], wg_all, wu_all, wd_all, lat_row(TM_FFN),
                  final_g=None if need_ctx else final_g)

    return xl.reshape(bsz, s_len, d)
```
